```python
import math
import jax, jax.numpy as jnp
from jax import lax
import numpy as np

D_MODEL = 2048
BATCH = 4
SEQ = 4096
DEPTH = 4

GRID_W = 64
CTX_LEN = 256
Q_BLOCK = 128
EPS = 1e-6
ROPE_THETA = 10000.0
N_MOD = 6

A_HEADS = 8
A_KV_HEADS = 2
A_HEAD_DIM = 128
A_WIDTH = A_HEADS * A_HEAD_DIM
A_KV_WIDTH = A_KV_HEADS * A_HEAD_DIM
B_WIDTH = D_MODEL - A_WIDTH
S5_GROUP = 16
S5_GROUPS = B_WIDTH // S5_GROUP
S5_STATE = 64
EVEN_IN = A_WIDTH + 2 * A_KV_WIDTH + B_WIDTH
C_HEADS = 16
C_NOPE = 128
C_ROPE = 64
C_V = 128
C_QK = C_NOPE + C_ROPE
Q_LORA = 1536
KV_LORA = 512
ODD_IN = Q_LORA + KV_LORA + C_ROPE
D_FF = 5632
N_EXPERTS = 8
TOP_K = 2
D_FF_EXPERT = 4096

kernel_name = 'hybrid_gqa_s5_mla_moe_diffusion_trunk'


def rms_norm(t, g):
    tf = t.astype(jnp.float32)
    y = tf * lax.rsqrt(jnp.mean(tf * tf, axis=-1, keepdims=True) + EPS)
    return (y * g.astype(jnp.float32)).astype(t.dtype)


def axial_rope(n_rows, rot_dim):
    n_freq = rot_dim // 4
    inv_freq = ROPE_THETA ** (-jnp.arange(n_freq, dtype=jnp.float32) / n_freq)
    rows = jnp.repeat(jnp.arange(n_rows, dtype=jnp.float32), GRID_W)
    cols = jnp.tile(jnp.arange(GRID_W, dtype=jnp.float32), n_rows)
    ang = jnp.concatenate([rows[:, None] * inv_freq, cols[:, None] * inv_freq], -1)
    return jnp.cos(ang), jnp.sin(ang)


def apply_rope(t, cos, sin):
    d2 = t.shape[-1] // 2
    shp = (1, t.shape[1]) + (1,) * (t.ndim - 3) + (d2,)
    cs, sn = cos.reshape(shp), sin.reshape(shp)
    t1 = t[..., :d2].astype(jnp.float32)
    t2 = t[..., d2:].astype(jnp.float32)
    return jnp.concatenate([t1 * cs - t2 * sn, t1 * sn + t2 * cs], -1).astype(t.dtype)


def attend(q, k, v, scale):
    s = jnp.einsum('bqhgd,bshd->bhgqs', q, k, preferred_element_type=jnp.float32) * scale
    p = jax.nn.softmax(s, axis=-1).astype(v.dtype)
    return jnp.einsum('bhgqs,bshd->bqhgd', p, v)


def blocked_attend(q, k, v, scale):
    b, l = q.shape[0], q.shape[1]
    nblk = l // Q_BLOCK
    qb = q.reshape((b, nblk, Q_BLOCK) + q.shape[2:]).swapaxes(0, 1)
    out = lax.map(lambda qi: attend(qi, k, v, scale), qb)
    return out.swapaxes(0, 1).reshape((b, l) + out.shape[3:])


def _lin_rec(e1, e2):
    a1, b1 = e1
    a2, b2 = e2
    return a1 * a2, a2 * b1 + b2


def s5_discretize(lam_re, lam_im, log_dt, b_re, b_im):
    lam = lax.complex(jnp.minimum(lam_re.astype(jnp.float32), -1e-4), lam_im.astype(jnp.float32))
    dt = jnp.exp(log_dt.astype(jnp.float32))[:, None]
    lam_bar = jnp.exp(lam * dt)
    bmat = lax.complex(b_re.astype(jnp.float32), b_im.astype(jnp.float32))
    b_bar = ((lam_bar - 1.0) / lam)[..., None] * bmat
    return lam_bar, b_bar


def s5_states(ug, lam_bar, b_bar, h0, reverse):
    bu = jnp.einsum('blgc,gpc->blgp', ug.astype(jnp.float32).astype(jnp.complex64), b_bar)
    if reverse:
        bu = jnp.flip(bu, 1)
    if h0 is not None:
        bu = bu.at[:, 0].add(lam_bar * h0)
    a = jnp.broadcast_to(lam_bar, bu.shape)
    _, h = lax.associative_scan(_lin_rec, (a, bu), axis=1)
    return jnp.flip(h, 1) if reverse else h


def s5_readout(h, cmat):
    y = jnp.real(jnp.einsum('blgp,gcp->blgc', h, cmat))
    return y.reshape(h.shape[0], h.shape[1], B_WIDTH)


def s5_mixer(u_ctx, u_lat, lam_re, lam_im, log_dt, b_re, b_im, c_re, c_im, d_skip, w_glu, need_ctx):
    bsz = u_lat.shape[0]
    ug_ctx = u_ctx.reshape(bsz, u_ctx.shape[1], S5_GROUPS, S5_GROUP)
    ug_lat = u_lat.reshape(bsz, u_lat.shape[1], S5_GROUPS, S5_GROUP)
    dsk = d_skip.astype(jnp.float32)
    y_lat = dsk * u_lat.astype(jnp.float32)
    y_ctx = dsk * u_ctx.astype(jnp.float32)
    for direction, reverse in ((0, False), (1, True)):
        lam_bar, b_bar = s5_discretize(lam_re[direction], lam_im[direction], log_dt[direction],
                                       b_re[direction], b_im[direction])
        cmat = lax.complex(c_re[direction].astype(jnp.float32), c_im[direction].astype(jnp.float32))
        h_ctx = s5_states(ug_ctx, lam_bar, b_bar, None, reverse)
        h0 = h_ctx[:, 0] if reverse else h_ctx[:, -1]
        h_lat = s5_states(ug_lat, lam_bar, b_bar, h0, reverse)
        y_lat = y_lat + s5_readout(h_lat, cmat)
        if need_ctx:
            y_ctx = y_ctx + s5_readout(h_ctx, cmat)

    def glu(y, dtype):
        y = jax.nn.gelu(y).astype(dtype)
        return y * jax.nn.sigmoid(y @ w_glu)

    out_lat = glu(y_lat, u_lat.dtype)
    out_ctx = glu(y_ctx, u_ctx.dtype) if need_ctx else None
    return out_ctx, out_lat


def _split_even(p):
    b, l, _ = p.shape
    q, k, v, u = jnp.split(p, [A_WIDTH, A_WIDTH + A_KV_WIDTH, A_WIDTH + 2 * A_KV_WIDTH], axis=-1)
    q = q.reshape(b, l, A_KV_HEADS, A_HEADS // A_KV_HEADS, A_HEAD_DIM)
    k = k.reshape(b, l, A_KV_HEADS, A_HEAD_DIM)
    v = v.reshape(b, l, A_KV_HEADS, A_HEAD_DIM)
    return q, k, v, u


def even_mixer(h_ctx, h_lat, w_in, qn_g, kn_g, lam_re, lam_im, log_dt, b_re, b_im, c_re, c_im,
               d_skip, w_glu, w_out, cos, sin, need_ctx):
    q_c, k_c, v_c, u_c = _split_even(h_ctx @ w_in)
    q_l, k_l, v_l, u_l = _split_even(h_lat @ w_in)
    q_c, k_c = rms_norm(q_c, qn_g), rms_norm(k_c, kn_g)
    q_l = apply_rope(rms_norm(q_l, qn_g), cos, sin)
    k_l = apply_rope(rms_norm(k_l, kn_g), cos, sin)
    k_all = jnp.concatenate([k_c, k_l], axis=1)
    v_all = jnp.concatenate([v_c, v_l], axis=1)
    scale = A_HEAD_DIM ** -0.5
    a_lat = blocked_attend(q_l, k_all, v_all, scale)
    s_ctx, s_lat = s5_mixer(u_c, u_l, lam_re, lam_im, log_dt, b_re, b_im, c_re, c_im, d_skip, w_glu, need_ctx)

    def merge(a, s):
        return jnp.concatenate([a.reshape(a.shape[0], a.shape[1], A_WIDTH), s], axis=-1) @ w_out

    o_lat = merge(a_lat, s_lat)
    o_ctx = merge(attend(q_c, k_c, v_c, scale), s_ctx) if need_ctx else None
    return o_ctx, o_lat


def odd_mixer(h_ctx, h_lat, w_in, q_lora_g, kv_lora_g, w_uq, w_ukv, qn_g, kn_g, w_o, cos, sin, need_ctx):
    def project(h, rope):
        b, l, _ = h.shape
        cq, ckv, k_pe = jnp.split(h @ w_in, [Q_LORA, Q_LORA + KV_LORA], axis=-1)
        q = (rms_norm(cq, q_lora_g) @ w_uq).reshape(b, l, C_HEADS, C_QK)
        kv = (rms_norm(ckv, kv_lora_g) @ w_ukv).reshape(b, l, C_HEADS, C_NOPE + C_V)
        k_nope, v = kv[..., :C_NOPE], kv[..., C_NOPE:]
        q_nope = rms_norm(q[..., :C_NOPE], qn_g[:C_NOPE])
        q_pe = rms_norm(q[..., C_NOPE:], qn_g[C_NOPE:])
        k_nope = rms_norm(k_nope, kn_g[:C_NOPE])
        k_pe = rms_norm(k_pe, kn_g[C_NOPE:])
        if rope:
            q_pe = apply_rope(q_pe, cos, sin)
            k_pe = apply_rope(k_pe, cos, sin)
        q = jnp.concatenate([q_nope, q_pe], axis=-1)[:, :, :, None, :]
        k = jnp.concatenate([k_nope, jnp.broadcast_to(k_pe[:, :, None, :], (b, l, C_HEADS, C_ROPE))], axis=-1)
        return q, k, v

    q_c, k_c, v_c = project(h_ctx, False)
    q_l, k_l, v_l = project(h_lat, True)
    scale = C_QK ** -0.5
    k_all = jnp.concatenate([k_c, k_l], axis=1)
    v_all = jnp.concatenate([v_c, v_l], axis=1)
    o_lat = blocked_attend(q_l, k_all, v_all, scale)
    o_lat = o_lat.reshape(o_lat.shape[0], o_lat.shape[1], C_HEADS * C_V) @ w_o
    o_ctx = None
    if need_ctx:
        o_ctx = attend(q_c, k_c, v_c, scale)
        o_ctx = o_ctx.reshape(o_ctx.shape[0], o_ctx.shape[1], C_HEADS * C_V) @ w_o
    return o_ctx, o_lat


def swiglu(h, w1, w3, w2):
    return (jax.nn.silu(h @ w1) * (h @ w3)) @ w2


def moe(h, w_router, w1, w3, w2):
    logits = jnp.einsum('bld,de->ble', h, w_router, preferred_element_type=jnp.float32)
    top_v, top_i = lax.top_k(logits, TOP_K)
    gates = jax.nn.softmax(top_v, axis=-1)
    combine = jnp.sum(jax.nn.one_hot(top_i, N_EXPERTS, dtype=jnp.float32) * gates[..., None], axis=-2)
    combine = combine.astype(h.dtype)
    out = jnp.zeros_like(h)
    for e in range(N_EXPERTS):
        out = out + combine[..., e:e + 1] * swiglu(h, w1[e], w3[e], w2[e])
    return out


def _modulation(cond, w, b):
    m = jnp.einsum('bd,de->be', jax.nn.silu(cond), w) + b
    return m.reshape(cond.shape[0], N_MOD, -1).swapaxes(0, 1)[:, :, None, :]


def setup_inputs(seed: int = 0) -> dict:
    key = jax.random.key(seed)
    keys = iter(jax.random.split(key, 48))
    n_even = (DEPTH + 1) // 2
    n_odd = DEPTH // 2
    f32 = jnp.float32

    def nrm(shape, scale):
        return jax.random.normal(next(keys), shape, f32) * scale

    def gain(shape):
        return 1.0 + 0.02 * jax.random.normal(next(keys), shape, f32)

    d = D_MODEL
    inp = {}
    inp['x'] = nrm((BATCH, SEQ, d), 1.0)
    inp['c'] = nrm((BATCH, d), 1.0)
    inp['ctx'] = nrm((BATCH, CTX_LEN, d), 1.0)
    inp['c_ctx'] = nrm((d,), 1.0)
    inp['w_mod'] = nrm((DEPTH, d, N_MOD * d), 0.5 * d ** -0.5)
    inp['b_mod'] = nrm((DEPTH, N_MOD * d), 0.02)
    inp['norm1_g'] = gain((DEPTH, d))
    inp['norm2_g'] = gain((DEPTH, d))
    inp['ev_w_in'] = nrm((n_even, d, EVEN_IN), d ** -0.5)
    inp['ev_qn_g'] = gain((n_even, A_HEAD_DIM))
    inp['ev_kn_g'] = gain((n_even, A_HEAD_DIM))
    inp['s5_lam_re'] = -0.5 + nrm((n_even, 2, S5_GROUPS, S5_STATE), 0.01)
    inp['s5_lam_im'] = (math.pi * jnp.arange(S5_STATE, dtype=f32)) + nrm((n_even, 2, S5_GROUPS, S5_STATE), 0.01)
    inp['s5_log_dt'] = jax.random.uniform(next(keys), (n_even, 2, S5_GROUPS), f32,
                                          minval=math.log(1e-3), maxval=math.log(1e-1))
    inp['s5_b_re'] = nrm((n_even, 2, S5_GROUPS, S5_STATE, S5_GROUP), (2 * S5_GROUP) ** -0.5)
    inp['s5_b_im'] = nrm((n_even, 2, S5_GROUPS, S5_STATE, S5_GROUP), (2 * S5_GROUP) ** -0.5)
    inp['s5_c_re'] = nrm((n_even, 2, S5_GROUPS, S5_GROUP, S5_STATE), (2 * S5_STATE) ** -0.5)
    inp['s5_c_im'] = nrm((n_even, 2, S5_GROUPS, S5_GROUP, S5_STATE), (2 * S5_STATE) ** -0.5)
    inp['s5_d'] = nrm((n_even, B_WIDTH), 1.0)
    inp['ev_w_glu'] = nrm((n_even, B_WIDTH, B_WIDTH), B_WIDTH ** -0.5)
    inp['ev_w_out'] = nrm((n_even, d, d), d ** -0.5)
    inp['ffn_w1'] = nrm((n_even, d, D_FF), d ** -0.5)
    inp['ffn_w3'] = nrm((n_even, d, D_FF), d ** -0.5)
    inp['ffn_w2'] = nrm((n_even, D_FF, d), D_FF ** -0.5)
    inp['od_w_in'] = nrm((n_odd, d, ODD_IN), d ** -0.5)
    inp['od_q_lora_g'] = gain((n_odd, Q_LORA))
    inp['od_kv_lora_g'] = gain((n_odd, KV_LORA))
    inp['od_w_uq'] = nrm((n_odd, Q_LORA, C_HEADS * C_QK), Q_LORA ** -0.5)
    inp['od_w_ukv'] = nrm((n_odd, KV_LORA, C_HEADS * (C_NOPE + C_V)), KV_LORA ** -0.5)
    inp['od_qn_g'] = gain((n_odd, C_QK))
    inp['od_kn_g'] = gain((n_odd, C_QK))
    inp['od_w_o'] = nrm((n_odd, C_HEADS * C_V, d), (C_HEADS * C_V) ** -0.5)
    inp['moe_router'] = nrm((n_odd, d, N_EXPERTS), d ** -0.5)
    inp['moe_w1'] = nrm((n_odd, N_EXPERTS, d, D_FF_EXPERT), d ** -0.5)
    inp['moe_w3'] = nrm((n_odd, N_EXPERTS, d, D_FF_EXPERT), d ** -0.5)
    inp['moe_w2'] = nrm((n_odd, N_EXPERTS, D_FF_EXPERT, d), D_FF_EXPERT ** -0.5)
    return inp


def reference(x, c, ctx, c_ctx, w_mod, b_mod, norm1_g, norm2_g,
              ev_w_in, ev_qn_g, ev_kn_g, s5_lam_re, s5_lam_im, s5_log_dt, s5_b_re, s5_b_im,
              s5_c_re, s5_c_im, s5_d, ev_w_glu, ev_w_out, ffn_w1, ffn_w3, ffn_w2,
              od_w_in, od_q_lora_g, od_kv_lora_g, od_w_uq, od_w_ukv, od_qn_g, od_kn_g, od_w_o,
              moe_router, moe_w1, moe_w3, moe_w2):
    n_rows = x.shape[1] // GRID_W
    cos_a, sin_a = axial_rope(n_rows, A_HEAD_DIM)
    cos_c, sin_c = axial_rope(n_rows, C_ROPE)
    for i in range(DEPTH):
        need_ctx = i < DEPTH - 1
        j = i // 2
        sh1, sc1, g1, sh2, sc2, g2 = _modulation(c, w_mod[i], b_mod[i])
        csh1, csc1, cg1, csh2, csc2, cg2 = _modulation(c_ctx[None], w_mod[i], b_mod[i])
        h_lat = rms_norm(x, norm1_g[i]) * (1.0 + sc1) + sh1
        h_ctx = rms_norm(ctx, norm1_g[i]) * (1.0 + csc1) + csh1
        if i % 2 == 0:
            o_ctx, o_lat = even_mixer(h_ctx, h_lat, ev_w_in[j], ev_qn_g[j], ev_kn_g[j],
                                      s5_lam_re[j], s5_lam_im[j], s5_log_dt[j], s5_b_re[j], s5_b_im[j],
                                      s5_c_re[j], s5_c_im[j], s5_d[j], ev_w_glu[j], ev_w_out[j],
                                      cos_a, sin_a, need_ctx)
        else:
            o_ctx, o_lat = odd_mixer(h_ctx, h_lat, od_w_in[j], od_q_lora_g[j], od_kv_lora_g[j],
                                     od_w_uq[j], od_w_ukv[j], od_qn_g[j], od_kn_g[j], od_w_o[j],
                                     cos_c, sin_c, need_ctx)
        x = x + g1 * o_lat
        if need_ctx:
            ctx = ctx + cg1 * o_ctx
        h_lat = rms_norm(x, norm2_g[i]) * (1.0 + sc2) + sh2
        if i % 2 == 0:
            x = x + g2 * swiglu(h_lat, ffn_w1[j], ffn_w3[j], ffn_w2[j])
        else:
            x = x + g2 * moe(h_lat, moe_router[j], moe_w1[j], moe_w3[j], moe_w2[j])
        if need_ctx:
            h_ctx = rms_norm(ctx, norm2_g[i]) * (1.0 + csc2) + csh2
            if i % 2 == 0:
                ctx = ctx + cg2 * swiglu(h_ctx, ffn_w1[j], ffn_w3[j], ffn_w2[j])
            else:
                ctx = ctx + cg2 * moe(h_ctx, moe_router[j], moe_w1[j], moe_w3[j], moe_w2[j])
    return x
```

```python
import functools
import math

import jax
import jax.numpy as jnp
from jax import lax
from jax.experimental import pallas as pl
from jax.experimental.pallas import tpu as pltpu

F32 = jnp.float32
BF16 = jnp.bfloat16
HIGHEST = lax.Precision.HIGHEST

EPS = 1e-6
ROPE_THETA = 10000.0
GRID_W = 64
N_MOD = 6
ROW_TILE = 256
LANE = 128
VMEM_LIMIT = 56 * 1024 * 1024

A_HEADS, A_KV_HEADS, A_HEAD_DIM = 8, 2, 128
S5_GROUP, S5_STATE = 16, 64
S5_CHUNK = 32
C_HEADS, C_NOPE, C_ROPE, C_V = 16, 128, 64, 128
Q_LORA, KV_LORA = 1536, 512
N_EXPERTS = 8
MOE_TILE = 512
FFN_TILE_M, FFN_TILE_F = 512, 512


def _cparams(*sem):
    return pltpu.CompilerParams(dimension_semantics=sem, vmem_limit_bytes=VMEM_LIMIT)


def _const_spec(shape):
    nd = len(shape)
    return pl.BlockSpec(shape, lambda *_: (0,) * nd)


class _Layout:
    def __init__(self, batch, ctx_len, seq):
        assert ctx_len == ROW_TILE and seq % ROW_TILE == 0
        self.batch, self.ctx_len, self.seq = batch, ctx_len, seq
        self.n_ctx = batch * ctx_len
        self.n_tok = batch * (ctx_len + seq)
        self.per_batch = ctx_len + seq
        self.ctx_tiles = self.n_ctx // ROW_TILE
        self.lat_tiles_per_batch = seq // ROW_TILE
        self.n_tiles = self.n_tok // ROW_TILE

    def mod_row(self, t, tile=ROW_TILE):
        ctx_tiles = self.n_ctx // tile
        per_b = self.seq // tile
        return jnp.where(t < ctx_tiles, self.batch, (t - ctx_tiles) // per_b)

    def batch_of(self, t):
        return jnp.where(t < self.ctx_tiles, t, (t - self.ctx_tiles) // self.lat_tiles_per_batch)

    def pos_tile(self, t):
        return jnp.where(t < self.ctx_tiles, 0, 1 + (t - self.ctx_tiles) % self.lat_tiles_per_batch)


def _rms(x, gain):
    return x * lax.rsqrt(jnp.mean(x * x, axis=-1, keepdims=True) + EPS) * gain


def _norm_mod(x, gain, shift, scale):
    return _rms(x, gain) * (1.0 + scale) + shift


def _mod_kernel(cond_ref, w_ref, b_ref, o_ref):
    cnd = cond_ref[...]
    act = cnd * jax.nn.sigmoid(cnd)
    o_ref[...] = jnp.dot(act, w_ref[...], preferred_element_type=F32, precision=HIGHEST) + b_ref[...]


def _modulation(cond, w_mod, b_mod):
    depth, d, n = w_mod.shape
    tn = 1024
    rows = cond.shape[0]
    out = pl.pallas_call(
        _mod_kernel,
        grid=(depth, n // tn),
        in_specs=[
            pl.BlockSpec((rows, d), lambda l, j: (0, 0)),
            pl.BlockSpec((None, d, tn), lambda l, j: (l, 0, j)),
            pl.BlockSpec((None, 1, tn), lambda l, j: (l, 0, j)),
        ],
        out_specs=pl.BlockSpec((None, rows, tn), lambda l, j: (l, 0, j)),
        out_shape=jax.ShapeDtypeStruct((depth, rows, n), F32),
        compiler_params=_cparams("parallel", "parallel"),
        name="modulation",
    )(cond, w_mod, b_mod.reshape(depth, 1, n))
    return out.reshape(depth, rows, N_MOD, d)


def _rope_tables(lay, rot_dim):
    n_rows = lay.seq // GRID_W
    n_freq = rot_dim // 4
    inv_freq = ROPE_THETA ** (-jnp.arange(n_freq, dtype=F32) / n_freq)
    rows = jnp.repeat(jnp.arange(n_rows, dtype=F32), GRID_W)
    cols = jnp.tile(jnp.arange(GRID_W, dtype=F32), n_rows)
    ang = jnp.concatenate([rows[:, None] * inv_freq, cols[:, None] * inv_freq], -1)
    cs, sn = jnp.cos(ang), jnp.sin(ang)
    cos_f = jnp.concatenate([cs, cs], -1)
    sin_f = jnp.concatenate([-sn, sn], -1)
    cos_f = jnp.concatenate([jnp.ones((lay.ctx_len, rot_dim), F32), cos_f], 0)
    sin_f = jnp.concatenate([jnp.zeros((lay.ctx_len, rot_dim), F32), sin_f], 0)
    return cos_f, sin_f


def _even_front_kernel(x_ref, mod_ref, ng_ref, w_ref, qg_ref, kg_ref, cos_ref, sin_ref,
                       q_ref, k_ref, v_ref, u_ref):
    h = _norm_mod(x_ref[...], ng_ref[...], mod_ref[0:1, :], mod_ref[1:2, :])
    p = jnp.dot(h.astype(BF16), w_ref[...], preferred_element_type=F32)
    cos, sin = cos_ref[...], sin_ref[...]
    hd = A_HEAD_DIM
    a_w, kv_w = A_HEADS * hd, A_KV_HEADS * hd

    def head(t, gain):
        t = _rms(t, gain)
        return t * cos + pltpu.roll(t, hd // 2, 1) * sin

    q_scale = hd ** -0.5
    for i in range(A_HEADS):
        q_ref[:, i * hd:(i + 1) * hd] = (head(p[:, i * hd:(i + 1) * hd], qg_ref[...]) * q_scale).astype(BF16)
    for i in range(A_KV_HEADS):
        lo = a_w + i * hd
        k_ref[:, i * hd:(i + 1) * hd] = head(p[:, lo:lo + hd], kg_ref[...]).astype(BF16)
    v_ref[...] = p[:, a_w + kv_w:a_w + 2 * kv_w].astype(BF16)
    u_ref[...] = p[:, a_w + 2 * kv_w:].astype(BF16)


def _even_front(lay, xs, mod, norm_g, w_in, qn_g, kn_g, cos_f, sin_f):
    d = xs.shape[1]
    hd = A_HEAD_DIM
    a_w, kv_w = A_HEADS * hd, A_KV_HEADS * hd
    b_w = w_in.shape[1] - a_w - 2 * kv_w
    pb = lambda t: (lay.batch_of(t), lay.pos_tile(t), 0)
    shp = lambda w: jax.ShapeDtypeStruct((lay.batch, lay.per_batch, w), BF16)
    return pl.pallas_call(
        _even_front_kernel,
        grid=(lay.n_tiles,),
        in_specs=[
            pl.BlockSpec((ROW_TILE, d), lambda t: (t, 0)),
            pl.BlockSpec((None, N_MOD, d), lambda t: (lay.mod_row(t), 0, 0)),
            _const_spec((1, d)),
            _const_spec(w_in.shape),
            _const_spec((1, hd)),
            _const_spec((1, hd)),
            pl.BlockSpec((ROW_TILE, hd), lambda t: (lay.pos_tile(t), 0)),
            pl.BlockSpec((ROW_TILE, hd), lambda t: (lay.pos_tile(t), 0)),
        ],
        out_specs=[
            pl.BlockSpec((None, ROW_TILE, a_w), pb),
            pl.BlockSpec((None, ROW_TILE, kv_w), pb),
            pl.BlockSpec((None, ROW_TILE, kv_w), pb),
            pl.BlockSpec((None, ROW_TILE, b_w), pb),
        ],
        out_shape=[shp(a_w), shp(kv_w), shp(kv_w), shp(b_w)],
        compiler_params=_cparams("parallel"),
        name="even_front",
    )(xs, mod, norm_g.reshape(1, d), w_in, qn_g.reshape(1, hd), kn_g.reshape(1, hd), cos_f, sin_f)


def _attn_kernel(q_ref, k_ref, v_ref, o_ref, m_sc, l_sc, acc_sc, *, group, dk, dv, ctx_len, kv_chunk, n_chunks):
    tq = q_ref.shape[0]
    q = jnp.concatenate([q_ref[:, g * dk:(g + 1) * dk] for g in range(group)], axis=0)
    m_sc[...] = jnp.full(m_sc.shape, -jnp.inf, F32)
    l_sc[...] = jnp.zeros(l_sc.shape, F32)
    acc_sc[...] = jnp.zeros(acc_sc.shape, F32)

    def chunk(start, size):
        k = k_ref[pl.ds(start, size), :]
        v = v_ref[pl.ds(start, size), :]
        s = lax.dot_general(q, k, (((1,), (1,)), ((), ())), preferred_element_type=F32)
        m_prev = m_sc[...]
        m_new = jnp.maximum(m_prev, jnp.max(s, axis=-1, keepdims=True))
        alpha = jnp.exp(m_prev - m_new)
        p = jnp.exp(s - m_new)
        l_sc[...] = alpha * l_sc[...] + jnp.sum(p, axis=-1, keepdims=True)
        acc_sc[...] = alpha * acc_sc[...] + jnp.dot(p.astype(BF16), v, preferred_element_type=F32)
        m_sc[...] = m_new

    chunk(0, ctx_len)

    @pl.when(pl.program_id(2) > 0)
    def _():
        def body(c, carry):
            chunk(pl.multiple_of(ctx_len + c * kv_chunk, kv_chunk), kv_chunk)
            return carry
        lax.fori_loop(0, n_chunks, body, 0)

    out = acc_sc[...] / l_sc[...]
    for g in range(group):
        o_ref[:, g * dv:(g + 1) * dv] = out[g * tq:(g + 1) * tq].astype(o_ref.dtype)


def _attention(lay, q, k, v, *, kv_heads, group, dk, dv, kv_chunk=512):
    b, p_len, _ = q.shape
    assert lay.ctx_len == ROW_TILE and lay.seq % kv_chunk == 0
    kern = functools.partial(_attn_kernel, group=group, dk=dk, dv=dv, ctx_len=lay.ctx_len,
                             kv_chunk=kv_chunk, n_chunks=lay.seq // kv_chunk)
    return pl.pallas_call(
        kern,
        grid=(b, kv_heads, p_len // ROW_TILE),
        in_specs=[
            pl.BlockSpec((None, ROW_TILE, group * dk), lambda bi, h, t: (bi, t, h)),
            pl.BlockSpec((None, p_len, dk), lambda bi, h, t: (bi, 0, h)),
            pl.BlockSpec((None, p_len, dv), lambda bi, h, t: (bi, 0, h)),
        ],
        out_specs=pl.BlockSpec((None, ROW_TILE, group * dv), lambda bi, h, t: (bi, t, h)),
        out_shape=jax.ShapeDtypeStruct((b, p_len, kv_heads * group * dv), BF16),
        scratch_shapes=[
            pltpu.VMEM((group * ROW_TILE, 1), F32),
            pltpu.VMEM((group * ROW_TILE, 1), F32),
            pltpu.VMEM((group * ROW_TILE, dv), F32),
        ],
        compiler_params=_cparams("parallel", "parallel", "arbitrary"),
        name="attention",
    )(q, k, v)


def _s5_operators(lam_re, lam_im, log_dt, b_re, b_im, c_re, c_im, chunk):
    g, p_dim, cg = b_re.shape
    lam = lax.complex(jnp.minimum(lam_re.astype(F32), -1e-4), lam_im.astype(F32))
    dt = jnp.exp(log_dt.astype(F32))[:, None]
    lam_dt = lam * dt
    lam_bar = jnp.exp(lam_dt)
    b_bar = ((lam_bar - 1.0) / lam)[..., None] * lax.complex(b_re.astype(F32), b_im.astype(F32))
    cmat = lax.complex(c_re.astype(F32), c_im.astype(F32))
    steps = jnp.arange(chunk + 1, dtype=F32)
    pw = jnp.exp(lam_dt[None] * steps[:, None, None].astype(jnp.complex64))
    kern = jnp.real(jnp.einsum('gcp,tgp,gpi->tgic', cmat, pw[:chunk], b_bar))
    s_idx = jnp.arange(chunk)[:, None]
    t_idx = jnp.arange(chunk)[None, :]
    lag = t_idx - s_idx
    toe = kern[jnp.clip(lag, 0, chunk - 1)]
    toe = jnp.where((lag >= 0)[:, :, None, None, None], toe, 0.0)
    intra = toe.transpose(2, 0, 3, 1, 4).reshape(g, chunk * cg, chunk * cg)
    st = pw[chunk - 1 - jnp.arange(chunk)][..., None] * b_bar[None]
    st = st.transpose(1, 0, 3, 2).reshape(g, chunk * cg, p_dim)
    so = cmat[None] * pw[1:chunk + 1][:, :, None, :]
    so = so.transpose(1, 3, 0, 2).reshape(g, p_dim, chunk * cg)
    decay = pw[chunk]
    return (intra, jnp.real(st), jnp.imag(st), jnp.real(so), -jnp.imag(so), jnp.real(decay), jnp.imag(decay))


def _flip_time(m, chunk, cg, axes):
    for ax in axes:
        shp = m.shape
        m = m.reshape(shp[:ax] + (chunk, cg) + shp[ax + 1:])
        m = jnp.flip(m, ax).reshape(shp)
    return m


def _s5_state_in_kernel(u_ref, w_ref, o0, o1, o2, o3):
    for gi in range(u_ref.shape[0]):
        u = u_ref[gi]
        for j, o in enumerate((o0, o1, o2, o3)):
            o[gi] = jnp.dot(u, w_ref[gi, j], preferred_element_type=F32)


def _s5_carry_kernel(sfr, sfi, sbr, sbi, lfr, lfi, lbr, lbi, hfr, hfi, hbr, hbi, *, n_chunks, ctx_chunks, rows):
    a_fr, a_fi, a_br, a_bi = lfr[...], lfi[...], lbr[...], lbi[...]

    def rows_at(kk):
        return pl.ds(kk, rows, stride=n_chunks)

    def body(i, carry):
        f_r, f_i, b_r, b_i = carry
        kf = i
        kb = jnp.where(i < ctx_chunks, ctx_chunks - 1 - i, n_chunks - 1 - (i - ctx_chunks))
        hfr[rows_at(kf), :] = f_r
        hfi[rows_at(kf), :] = f_i
        hbr[rows_at(kb), :] = b_r
        hbi[rows_at(kb), :] = b_i
        nf_r = a_fr * f_r - a_fi * f_i + sfr[rows_at(kf), :]
        nf_i = a_fr * f_i + a_fi * f_r + sfi[rows_at(kf), :]
        nb_r = a_br * b_r - a_bi * b_i + sbr[rows_at(kb), :]
        nb_i = a_br * b_i + a_bi * b_r + sbi[rows_at(kb), :]
        return nf_r, nf_i, nb_r, nb_i

    z = jnp.zeros(a_fr.shape, F32)
    lax.fori_loop(0, n_chunks, body, (z, z, z, z))


def _s5_out_kernel(u_ref, wi_ref, wo_ref, h0, h1, h2, h3, y_ref):
    for gi in range(u_ref.shape[0]):
        acc = jnp.dot(u_ref[gi], wi_ref[gi], preferred_element_type=F32)
        for j, h in enumerate((h0, h1, h2, h3)):
            acc += jnp.dot(h[gi].astype(BF16), wo_ref[gi, j], preferred_element_type=F32)
        y_ref[gi] = acc


def _s5_scan(lay, u, lam_re, lam_im, log_dt, b_re, b_im, c_re, c_im):
    bsz, p_len, width = u.shape
    t, cg, p_dim = S5_CHUNK, S5_GROUP, S5_STATE
    g = width // cg
    n_chunks = p_len // t
    ctx_chunks = lay.ctx_len // t
    tc = t * cg
    rows = bsz * n_chunks
    fwd = _s5_operators(lam_re[0], lam_im[0], log_dt[0], b_re[0], b_im[0], c_re[0], c_im[0], t)
    bwd = _s5_operators(lam_re[1], lam_im[1], log_dt[1], b_re[1], b_im[1], c_re[1], c_im[1], t)
    intra = (fwd[0] + _flip_time(bwd[0], t, cg, (1, 2))).astype(BF16)
    w_in = jnp.stack([fwd[1], fwd[2], _flip_time(bwd[1], t, cg, (1,)), _flip_time(bwd[2], t, cg, (1,))],
                     axis=1).astype(BF16)
    w_out = jnp.stack([fwd[3], fwd[4], _flip_time(bwd[3], t, cg, (2,)), _flip_time(bwd[4], t, cg, (2,))],
                      axis=1).astype(BF16)

    ur = u.reshape(bsz, n_chunks, t, g, cg).transpose(3, 0, 1, 2, 4).reshape(g, rows, tc)

    gb = 4
    state_shape = jax.ShapeDtypeStruct((g, rows, p_dim), F32)
    st_spec = pl.BlockSpec((gb, rows, p_dim), lambda i: (i, 0, 0))
    s_in = pl.pallas_call(
        _s5_state_in_kernel,
        grid=(g // gb,),
        in_specs=[pl.BlockSpec((gb, rows, tc), lambda i: (i, 0, 0)),
                  pl.BlockSpec((gb, 4, tc, p_dim), lambda i: (i, 0, 0, 0))],
        out_specs=[st_spec] * 4,
        out_shape=[state_shape] * 4,
        compiler_params=_cparams("parallel"),
        name="s5_state_in",
    )(ur, w_in)

    gb_c = 4
    crow = gb_c * bsz
    flat = lambda a: a.reshape(g * rows, p_dim)
    dec = [jnp.repeat(a, bsz, axis=0) for a in (fwd[5], fwd[6], bwd[5], bwd[6])]
    carry_spec = pl.BlockSpec((crow * n_chunks, p_dim), lambda i: (i, 0))
    h_in = pl.pallas_call(
        functools.partial(_s5_carry_kernel, n_chunks=n_chunks, ctx_chunks=ctx_chunks, rows=crow),
        grid=(g // gb_c,),
        in_specs=[carry_spec] * 4 + [pl.BlockSpec((crow, p_dim), lambda i: (i, 0))] * 4,
        out_specs=[carry_spec] * 4,
        out_shape=[jax.ShapeDtypeStruct((g * rows, p_dim), F32)] * 4,
        compiler_params=_cparams("parallel"),
        name="s5_carry",
    )(*[flat(a) for a in s_in], *dec)
    h_in = [a.reshape(g, rows, p_dim) for a in h_in]

    yr = pl.pallas_call(
        _s5_out_kernel,
        grid=(g // gb,),
        in_specs=[pl.BlockSpec((gb, rows, tc), lambda i: (i, 0, 0)),
                  pl.BlockSpec((gb, tc, tc), lambda i: (i, 0, 0)),
                  pl.BlockSpec((gb, 4, p_dim, tc), lambda i: (i, 0, 0, 0))] + [st_spec] * 4,
        out_specs=pl.BlockSpec((gb, rows, tc), lambda i: (i, 0, 0)),
        out_shape=jax.ShapeDtypeStruct((g, rows, tc), F32),
        compiler_params=_cparams("parallel"),
        name="s5_out",
    )(ur, intra, w_out, *h_in)
    return yr.reshape(g, bsz, n_chunks, t, cg).transpose(1, 2, 3, 0, 4).reshape(bsz, p_len, width)


def _gelu_tanh(x):
    return 0.5 * x * (1.0 + jnp.tanh(math.sqrt(2.0 / math.pi) * (x + 0.044715 * (x * x * x))))


def _even_back_kernel(a_ref, y_ref, u_ref, x_ref, mod_ref, d_ref, wg_ref, wo_ref, ng_ref, xo_ref, h_ref):
    y = d_ref[...] * u_ref[...].astype(F32) + y_ref[...]
    gl = _gelu_tanh(y)
    gate = jnp.dot(gl.astype(BF16), wg_ref[...], preferred_element_type=F32)
    s = gl * jax.nn.sigmoid(gate)
    a_w = a_ref.shape[1]
    o = jnp.dot(a_ref[...], wo_ref[0:a_w, :], preferred_element_type=F32)
    o += jnp.dot(s.astype(BF16), wo_ref[a_w:, :], preferred_element_type=F32)
    x_new = x_ref[...] + mod_ref[2:3, :] * o
    xo_ref[...] = x_new
    h_ref[...] = _norm_mod(x_new, ng_ref[...], mod_ref[3:4, :], mod_ref[4:5, :]).astype(BF16)


def _even_back(lay, attn, y_ssm, u, xs, mod, d_skip, w_glu, w_out, norm_g):
    n, d = xs.shape
    a_w, b_w = attn.shape[2], u.shape[2]
    pb = lambda t: (lay.batch_of(t), lay.pos_tile(t), 0)
    return pl.pallas_call(
        _even_back_kernel,
        grid=(lay.n_tiles,),
        in_specs=[
            pl.BlockSpec((None, ROW_TILE, a_w), pb),
            pl.BlockSpec((None, ROW_TILE, b_w), pb),
            pl.BlockSpec((None, ROW_TILE, b_w), pb),
            pl.BlockSpec((ROW_TILE, d), lambda t: (t, 0)),
            pl.BlockSpec((None, N_MOD, d), lambda t: (lay.mod_row(t), 0, 0)),
            _const_spec((1, b_w)),
            _const_spec(w_glu.shape),
            _const_spec(w_out.shape),
            _const_spec((1, d)),
        ],
        out_specs=[pl.BlockSpec((ROW_TILE, d), lambda t: (t, 0)),
                   pl.BlockSpec((ROW_TILE, d), lambda t: (t, 0))],
        out_shape=[jax.ShapeDtypeStruct((n, d), F32), jax.ShapeDtypeStruct((n, d), BF16)],
        compiler_params=_cparams("parallel"),
        name="even_back",
    )(attn, y_ssm, u, xs, mod, d_skip.reshape(1, b_w), w_glu, w_out, norm_g.reshape(1, d))


def _ffn_kernel(h_ref, x_ref, mod_ref, w1_ref, w3_ref, w2_ref, o_ref, acc_ref):
    j = pl.program_id(1)

    @pl.when(j == 0)
    def _():
        acc_ref[...] = jnp.zeros(acc_ref.shape, F32)

    h = h_ref[...]
    a = jnp.dot(h, w1_ref[...], preferred_element_type=F32)
    b = jnp.dot(h, w3_ref[...], preferred_element_type=F32)
    z = (a * jax.nn.sigmoid(a) * b).astype(BF16)
    acc_ref[...] += jnp.dot(z, w2_ref[...], preferred_element_type=F32)

    @pl.when(j == pl.num_programs(1) - 1)
    def _():
        o_ref[...] = x_ref[...] + mod_ref[5:6, :] * acc_ref[...]


def _ffn(lay, h, xs, mod, w1, w3, w2):
    n, d = xs.shape
    f = w1.shape[1]
    tm, tf = FFN_TILE_M, FFN_TILE_F
    assert n % tm == 0 and f % tf == 0 and lay.n_ctx % tm == 0 and lay.seq % tm == 0
    return pl.pallas_call(
        _ffn_kernel,
        grid=(n // tm, f // tf),
        in_specs=[
            pl.BlockSpec((tm, d), lambda i, j: (i, 0)),
            pl.BlockSpec((tm, d), lambda i, j: (i, 0)),
            pl.BlockSpec((None, N_MOD, d), lambda i, j: (lay.mod_row(i, tm), 0, 0)),
            pl.BlockSpec((d, tf), lambda i, j: (0, j)),
            pl.BlockSpec((d, tf), lambda i, j: (0, j)),
            pl.BlockSpec((tf, d), lambda i, j: (j, 0)),
        ],
        out_specs=pl.BlockSpec((tm, d), lambda i, j: (i, 0)),
        out_shape=jax.ShapeDtypeStruct((n, d), F32),
        scratch_shapes=[pltpu.VMEM((tm, d), F32)],
        compiler_params=_cparams("parallel", "arbitrary"),
        name="dense_swiglu",
    )(h, xs, mod, w1, w3, w2)


def _odd_front_a_kernel(x_ref, mod_ref, ng_ref, w_ref, qg_ref, kvg_ref, cq_ref, ckv_ref, kpe_ref):
    h = _norm_mod(x_ref[...], ng_ref[...], mod_ref[0:1, :], mod_ref[1:2, :])
    p = jnp.dot(h.astype(BF16), w_ref[...], preferred_element_type=F32)
    cq_ref[...] = _rms(p[:, :Q_LORA], qg_ref[...]).astype(BF16)
    ckv_ref[...] = _rms(p[:, Q_LORA:Q_LORA + KV_LORA], kvg_ref[...]).astype(BF16)
    kpe_ref[...] = p[:, Q_LORA + KV_LORA:]


def _swap_halves(t, width):
    half = width // 2
    if width == LANE:
        return pltpu.roll(t, half, 1)
    lane = lax.broadcasted_iota(jnp.int32, t.shape, 1)
    return jnp.where(lane % width < half, pltpu.roll(t, LANE - half, 1), pltpu.roll(t, half, 1))


def _segment_rms(t, width, seg_ones):
    ss = jnp.dot(t * t, seg_ones, preferred_element_type=F32, precision=HIGHEST)
    return t * lax.rsqrt(ss * (1.0 / width) + EPS)


def _odd_front_b_kernel(cq_ref, ckv_ref, kpe_ref, wq_ref, wkv_ref, qg_ref, kg_ref, cos_ref, sin_ref, seg_ref,
                        q_ref, k_ref, v_ref):
    q = jnp.dot(cq_ref[...], wq_ref[...], preferred_element_type=F32)
    kv = jnp.dot(ckv_ref[...], wkv_ref[...], preferred_element_type=F32)
    cos, sin, seg = cos_ref[...], sin_ref[...], seg_ref[...]
    qg_n, qg_r = qg_ref[:, :LANE], qg_ref[:, LANE:]
    kg_n, kg_r = kg_ref[:, :LANE], kg_ref[:, LANE:]
    scale = (C_NOPE + C_ROPE) ** -0.5
    dkp = 2 * LANE
    lane = lax.broadcasted_iota(jnp.int32, (q.shape[0], LANE), 1)

    def rope(t):
        return t * cos + _swap_halves(t, C_ROPE) * sin

    kpe = rope(_segment_rms(kpe_ref[...], C_ROPE, seg) * kg_r)
    kpe_even = jnp.where(lane < C_ROPE, kpe, 0.0).astype(BF16)
    kpe_odd = jnp.where(lane < C_ROPE, 0.0, pltpu.roll(kpe, C_ROPE, 1)).astype(BF16)
    rope_base = C_HEADS * C_NOPE
    for pair in range(C_HEADS // 2):
        qp = q[:, rope_base + pair * LANE:rope_base + (pair + 1) * LANE]
        qp = (rope(_segment_rms(qp, C_ROPE, seg) * qg_r) * scale).astype(BF16)
        for hh in (2 * pair, 2 * pair + 1):
            qn = _rms(q[:, hh * C_NOPE:(hh + 1) * C_NOPE], qg_n) * scale
            q_ref[:, hh * dkp:hh * dkp + LANE] = qn.astype(BF16)
            q_ref[:, hh * dkp + LANE:(hh + 1) * dkp] = qp
            kn = _rms(kv[:, hh * 2 * LANE:hh * 2 * LANE + C_NOPE], kg_n)
            k_ref[:, hh * dkp:hh * dkp + LANE] = kn.astype(BF16)
            k_ref[:, hh * dkp + LANE:(hh + 1) * dkp] = kpe_even if hh % 2 == 0 else kpe_odd
            v_ref[:, hh * C_V:(hh + 1) * C_V] = kv[:, hh * 2 * LANE + C_NOPE:(hh + 1) * 2 * LANE].astype(BF16)


def _odd_front(lay, xs, mod, norm_g, w_in, q_lora_g, kv_lora_g, w_uq, w_ukv, qn_g, kn_g, cos_f, sin_f):
    n, d = xs.shape
    assert C_NOPE == LANE and C_V == LANE and 2 * C_ROPE == LANE
    cq, ckv, kpe = pl.pallas_call(
        _odd_front_a_kernel,
        grid=(lay.n_tiles,),
        in_specs=[
            pl.BlockSpec((ROW_TILE, d), lambda t: (t, 0)),
            pl.BlockSpec((None, N_MOD, d), lambda t: (lay.mod_row(t), 0, 0)),
            _const_spec((1, d)),
            _const_spec(w_in.shape),
            _const_spec((1, Q_LORA)),
            _const_spec((1, KV_LORA)),
        ],
        out_specs=[pl.BlockSpec((ROW_TILE, Q_LORA), lambda t: (t, 0)),
                   pl.BlockSpec((ROW_TILE, KV_LORA), lambda t: (t, 0)),
                   pl.BlockSpec((ROW_TILE, LANE), lambda t: (t, 0))],
        out_shape=[jax.ShapeDtypeStruct((n, Q_LORA), BF16), jax.ShapeDtypeStruct((n, KV_LORA), BF16),
                   jax.ShapeDtypeStruct((n, LANE), F32)],
        compiler_params=_cparams("parallel"),
        name="odd_front_a",
    )(xs, mod, norm_g.reshape(1, d), w_in, q_lora_g.reshape(1, Q_LORA), kv_lora_g.reshape(1, KV_LORA))

    seg = (jnp.arange(LANE)[:, None] // C_ROPE == jnp.arange(LANE)[None, :] // C_ROPE).astype(F32)
    pair_gain = lambda g: jnp.concatenate([g[:C_NOPE], g[C_NOPE:], g[C_NOPE:]]).astype(F32).reshape(1, 2 * LANE)
    pb = lambda t: (lay.batch_of(t), lay.pos_tile(t), 0)
    shp = lambda w: jax.ShapeDtypeStruct((lay.batch, lay.per_batch, w), BF16)
    dkp = 2 * LANE
    return pl.pallas_call(
        _odd_front_b_kernel,
        grid=(lay.n_tiles,),
        in_specs=[
            pl.BlockSpec((ROW_TILE, Q_LORA), lambda t: (t, 0)),
            pl.BlockSpec((ROW_TILE, KV_LORA), lambda t: (t, 0)),
            pl.BlockSpec((ROW_TILE, LANE), lambda t: (t, 0)),
            _const_spec(w_uq.shape),
            _const_spec(w_ukv.shape),
            _const_spec((1, 2 * LANE)),
            _const_spec((1, 2 * LANE)),
            pl.BlockSpec((ROW_TILE, LANE), lambda t: (lay.pos_tile(t), 0)),
            pl.BlockSpec((ROW_TILE, LANE), lambda t: (lay.pos_tile(t), 0)),
            _const_spec((LANE, LANE)),
        ],
        out_specs=[pl.BlockSpec((None, ROW_TILE, C_HEADS * dkp), pb),
                   pl.BlockSpec((None, ROW_TILE, C_HEADS * dkp), pb),
                   pl.BlockSpec((None, ROW_TILE, C_HEADS * C_V), pb)],
        out_shape=[shp(C_HEADS * dkp), shp(C_HEADS * dkp), shp(C_HEADS * C_V)],
        compiler_params=_cparams("parallel"),
        name="odd_front_b",
    )(cq, ckv, kpe, w_uq, w_ukv, pair_gain(qn_g), pair_gain(kn_g), cos_f, sin_f, seg)


def _odd_back_kernel(a_ref, x_ref, mod_ref, wo_ref, ng_ref, wr_ref, xo_ref, h_ref, r_ref):
    o = jnp.dot(a_ref[...], wo_ref[...], preferred_element_type=F32)
    x_new = x_ref[...] + mod_ref[2:3, :] * o
    xo_ref[...] = x_new
    h = _norm_mod(x_new, ng_ref[...], mod_ref[3:4, :], mod_ref[4:5, :])
    h_ref[...] = h.astype(BF16)
    logits = jnp.dot(h, wr_ref[...], preferred_element_type=F32, precision=HIGHEST)
    lane = lax.broadcasted_iota(jnp.int32, logits.shape, 1).astype(F32)
    neg = jnp.float32(-jnp.inf)
    lg = jnp.where(lane < N_EXPERTS, logits, neg)
    m1 = jnp.max(lg, axis=-1, keepdims=True)
    i1 = jnp.min(jnp.where(lg == m1, lane, float(LANE)), axis=-1, keepdims=True)
    lg2 = jnp.where(lane == i1, neg, lg)
    m2 = jnp.max(lg2, axis=-1, keepdims=True)
    i2 = jnp.min(jnp.where(lg2 == m2, lane, float(LANE)), axis=-1, keepdims=True)
    e2 = jnp.exp(m2 - m1)
    g1 = 1.0 / (1.0 + e2)
    g2 = e2 / (1.0 + e2)
    r = jnp.where(lane == 0.0, i1, 0.0)
    r = jnp.where(lane == 1.0, i2, r)
    r = jnp.where(lane == 2.0, g1, r)
    r_ref[...] = jnp.where(lane == 3.0, g2, r)


def _odd_back(lay, attn, xs, mod, w_o, norm_g, w_router):
    n, d = xs.shape
    a_w = attn.shape[2]
    wr = jnp.zeros((d, LANE), F32).at[:, :N_EXPERTS].set(w_router.astype(F32))
    pb = lambda t: (lay.batch_of(t), lay.pos_tile(t), 0)
    row = lambda w: pl.BlockSpec((ROW_TILE, w), lambda t: (t, 0))
    return pl.pallas_call(
        _odd_back_kernel,
        grid=(lay.n_tiles,),
        in_specs=[
            pl.BlockSpec((None, ROW_TILE, a_w), pb),
            row(d),
            pl.BlockSpec((None, N_MOD, d), lambda t: (lay.mod_row(t), 0, 0)),
            _const_spec(w_o.shape),
            _const_spec((1, d)),
            _const_spec((d, LANE)),
        ],
        out_specs=[row(d), row(d), row(LANE)],
        out_shape=[jax.ShapeDtypeStruct((n, d), F32), jax.ShapeDtypeStruct((n, d), BF16),
                   jax.ShapeDtypeStruct((n, LANE), F32)],
        compiler_params=_cparams("parallel"),
        name="odd_back",
    )(attn, xs, mod, w_o, norm_g.reshape(1, d), wr)


def _moe_kernel(te_ref, tv_ref, h_ref, g_ref, w1_ref, w3_ref, w2_ref, o_ref, acc_ref):
    r, j = pl.program_id(0), pl.program_id(1)
    last = pl.num_programs(1) - 1
    valid = tv_ref[r] > 0

    @pl.when(j == 0)
    def _():
        acc_ref[...] = jnp.zeros(acc_ref.shape, F32)

    @pl.when(valid)
    def _():
        h = h_ref[...]
        a = jnp.dot(h, w1_ref[...], preferred_element_type=F32)
        b = jnp.dot(h, w3_ref[...], preferred_element_type=F32)
        z = (a * jax.nn.sigmoid(a) * b).astype(BF16)
        acc_ref[...] += jnp.dot(z, w2_ref[...], preferred_element_type=F32)

    @pl.when(j == last)
    def _():
        o_ref[...] = g_ref[...] * acc_ref[...]


def _moe_ffn(h_sorted, gate_sorted, tile_expert, tile_valid, w1, w3, w2):
    s_pad, d = h_sorted.shape
    f = w1.shape[2]
    tm, tf = MOE_TILE, 512
    n_tiles = s_pad // tm
    jf = lambda r, j, tv: jnp.where(tv[r] > 0, j, f // tf - 1)
    grid_spec = pltpu.PrefetchScalarGridSpec(
        num_scalar_prefetch=2,
        grid=(n_tiles, f // tf),
        in_specs=[
            pl.BlockSpec((tm, d), lambda r, j, te, tv: (r, 0)),
            pl.BlockSpec((tm, 1), lambda r, j, te, tv: (r, 0)),
            pl.BlockSpec((None, d, tf), lambda r, j, te, tv: (te[r], 0, jf(r, j, tv))),
            pl.BlockSpec((None, d, tf), lambda r, j, te, tv: (te[r], 0, jf(r, j, tv))),
            pl.BlockSpec((None, tf, d), lambda r, j, te, tv: (te[r], jf(r, j, tv), 0)),
        ],
        out_specs=pl.BlockSpec((tm, d), lambda r, j, te, tv: (r, 0)),
        scratch_shapes=[pltpu.VMEM((tm, d), F32)],
    )
    return pl.pallas_call(
        _moe_kernel,
        grid_spec=grid_spec,
        out_shape=jax.ShapeDtypeStruct((s_pad, d), F32),
        compiler_params=_cparams("arbitrary", "arbitrary"),
        name="moe_experts",
    )(tile_expert, tile_valid, h_sorted, gate_sorted, w1, w3, w2)


def _combine_kernel(x_ref, mod_ref, y0_ref, y1_ref, o_ref):
    o_ref[...] = x_ref[...] + mod_ref[5:6, :] * (y0_ref[...] + y1_ref[...])


def _moe_combine(lay, xs, mod, y0, y1):
    n, d = xs.shape
    row = pl.BlockSpec((ROW_TILE, d), lambda t: (t, 0))
    return pl.pallas_call(
        _combine_kernel,
        grid=(lay.n_tiles,),
        in_specs=[row, pl.BlockSpec((None, N_MOD, d), lambda t: (lay.mod_row(t), 0, 0)), row, row],
        out_specs=row,
        out_shape=jax.ShapeDtypeStruct((n, d), F32),
        compiler_params=_cparams("parallel"),
        name="moe_combine",
    )(xs, mod, y0, y1)


def _moe(lay, h, route, xs, mod, w1, w3, w2):
    n, d = xs.shape
    tm = MOE_TILE
    n_assign = 2 * n
    n_tiles = n_assign // tm + N_EXPERTS
    s_pad = n_tiles * tm
    expert = route[:, 0:2].astype(jnp.int32).reshape(n_assign)
    gate = route[:, 2:4].reshape(n_assign)
    onehot = (expert[:, None] == jnp.arange(N_EXPERTS)[None, :]).astype(jnp.int32)
    rank = jnp.take_along_axis(jnp.cumsum(onehot, axis=0) - onehot, expert[:, None], axis=1)[:, 0]
    count = jnp.sum(onehot, axis=0)
    padded = (count + tm - 1) // tm * tm
    end = jnp.cumsum(padded)
    start = end - padded
    slot = start[expert] + rank
    token = jnp.arange(n_assign, dtype=jnp.int32) // 2
    tok_of_slot = jnp.zeros((s_pad,), jnp.int32).at[slot].set(token)
    gate_of_slot = jnp.zeros((s_pad,), F32).at[slot].set(gate)
    tile_start = jnp.arange(n_tiles, dtype=jnp.int32) * tm
    tile_expert = jnp.minimum(jnp.sum((tile_start[:, None] >= end[None, :]).astype(jnp.int32), axis=1),
                              N_EXPERTS - 1).astype(jnp.int32)
    tile_valid = (tile_start < end[-1]).astype(jnp.int32)
    h_sorted = jnp.take(h, tok_of_slot, axis=0)
    y = _moe_ffn(h_sorted, gate_of_slot[:, None], tile_expert, tile_valid, w1, w3, w2)
    slots = slot.reshape(n, 2)
    y0 = jnp.take(y, slots[:, 0], axis=0)
    y1 = jnp.take(y, slots[:, 1], axis=0)
    return _moe_combine(lay, xs, mod, y0, y1)


def kernel(x, c, ctx, c_ctx, w_mod, b_mod, norm1_g, norm2_g, ev_w_in, ev_qn_g, ev_kn_g, s5_lam_re, s5_lam_im, s5_log_dt, s5_b_re, s5_b_im, s5_c_re, s5_c_im, s5_d, ev_w_glu, ev_w_out, ffn_w1, ffn_w3, ffn_w2, od_w_in, od_q_lora_g, od_kv_lora_g, od_w_uq, od_w_ukv, od_qn_g, od_kn_g, od_w_o, moe_router, moe_w1, moe_w3, moe_w2):
    bsz, seq, d = x.shape
    ctx_len = ctx.shape[1]
    depth = w_mod.shape[0]
    lay = _Layout(bsz, ctx_len, seq)

    cond = jnp.zeros((8, d), F32).at[:bsz].set(c).at[bsz].set(c_ctx)
    mods = _modulation(cond, w_mod, b_mod)

    cos_a, sin_a = _rope_tables(lay, A_HEAD_DIM)
    cos_c, sin_c = _rope_tables(lay, C_ROPE)
    cos_c = jnp.concatenate([cos_c, cos_c], -1)
    sin_c = jnp.concatenate([sin_c, sin_c], -1)

    nope_cols = (jnp.arange(C_HEADS)[:, None] * (C_NOPE + C_ROPE) + jnp.arange(C_NOPE)[None, :]).reshape(-1)
    rope_cols = (jnp.arange(C_HEADS)[:, None] * (C_NOPE + C_ROPE) + C_NOPE + jnp.arange(C_ROPE)[None, :]).reshape(-1)
    uq_cols = jnp.concatenate([nope_cols, rope_cols])

    xs = jnp.concatenate([ctx.reshape(bsz * ctx_len, d), x.reshape(bsz * seq, d)], axis=0)
    for i in range(depth):
        j = i // 2
        mod = mods[i]
        if i % 2 == 0:
            q, k, v, u = _even_front(lay, xs, mod, norm1_g[i], ev_w_in[j].astype(BF16), ev_qn_g[j], ev_kn_g[j],
                                     cos_a, sin_a)
            attn = _attention(lay, q, k, v, kv_heads=A_KV_HEADS, group=A_HEADS // A_KV_HEADS,
                              dk=A_HEAD_DIM, dv=A_HEAD_DIM)
            y_ssm = _s5_scan(lay, u, s5_lam_re[j], s5_lam_im[j], s5_log_dt[j], s5_b_re[j], s5_b_im[j],
                             s5_c_re[j], s5_c_im[j])
            xs, h = _even_back(lay, attn, y_ssm, u, xs, mod, s5_d[j], ev_w_glu[j].astype(BF16),
                               ev_w_out[j].astype(BF16), norm2_g[i])
            xs = _ffn(lay, h, xs, mod, ffn_w1[j].astype(BF16), ffn_w3[j].astype(BF16), ffn_w2[j].astype(BF16))
        else:
            w_in = jnp.pad(od_w_in[j], ((0, 0), (0, LANE - C_ROPE))).astype(BF16)
            q, k, v = _odd_front(lay, xs, mod, norm1_g[i], w_in, od_q_lora_g[j], od_kv_lora_g[j],
                                 od_w_uq[j][:, uq_cols].astype(BF16), od_w_ukv[j].astype(BF16),
                                 od_qn_g[j], od_kn_g[j], cos_c, sin_c)
            attn = _attention(lay, q, k, v, kv_heads=C_HEADS, group=1, dk=2 * LANE, dv=C_V)
            xs, h, route = _odd_back(lay, attn, xs, mod, od_w_o[j].astype(BF16), norm2_g[i], moe_router[j])
            xs = _moe(lay, h, route, xs, mod, moe_w1[j].astype(BF16), moe_w3[j].astype(BF16),
                      moe_w2[j].astype(BF16))
    return xs[bsz * ctx_len:].reshape(bsz, seq, d)
```

```python
import functools
import math

import jax
import jax.numpy as jnp
from jax import lax
from jax.experimental import pallas as pl
from jax.experimental.pallas import tpu as pltpu

F32 = jnp.float32
BF16 = jnp.bfloat16
HIGHEST = lax.Precision.HIGHEST

EPS = 1e-6
ROPE_THETA = 10000.0
GRID_W = 64
N_MOD = 6
ROW_TILE = 256
LANE = 128
VMEM_LIMIT = 56 * 1024 * 1024
LOG2E = 1.4426950408889634
V_PAD = 16

A_HEADS, A_KV_HEADS, A_HEAD_DIM = 8, 2, 128
S5_GROUP, S5_STATE = 16, 64
S5_CHUNK = LANE
C_HEADS, C_NOPE, C_ROPE, C_V = 16, 128, 64, 128
Q_LORA, KV_LORA = 1536, 512
N_EXPERTS = 8
MOE_TILE = 512
FFN_TILE_M, FFN_TILE_F = 512, 512


def _cparams(*sem):
    return pltpu.CompilerParams(dimension_semantics=sem, vmem_limit_bytes=VMEM_LIMIT)


def _const_spec(shape):
    nd = len(shape)
    return pl.BlockSpec(shape, lambda *_: (0,) * nd)


class _Layout:
    def __init__(self, batch, ctx_len, seq):
        assert ctx_len == ROW_TILE and seq % ROW_TILE == 0
        self.batch, self.ctx_len, self.seq = batch, ctx_len, seq
        self.n_ctx = batch * ctx_len
        self.n_tok = batch * (ctx_len + seq)
        self.per_batch = ctx_len + seq
        self.ctx_tiles = self.n_ctx // ROW_TILE
        self.lat_tiles_per_batch = seq // ROW_TILE
        self.n_tiles = self.n_tok // ROW_TILE

    def mod_row(self, t, tile=ROW_TILE):
        ctx_tiles = self.n_ctx // tile
        per_b = self.seq // tile
        return jnp.where(t < ctx_tiles, self.batch, (t - ctx_tiles) // per_b)

    def batch_of(self, t):
        return jnp.where(t < self.ctx_tiles, t, (t - self.ctx_tiles) // self.lat_tiles_per_batch)

    def pos_tile(self, t):
        return jnp.where(t < self.ctx_tiles, 0, 1 + (t - self.ctx_tiles) % self.lat_tiles_per_batch)

    def chunk_tile(self, t):
        return self.batch_of(t) * (self.per_batch // ROW_TILE) + self.pos_tile(t)


def _rms(x, gain):
    return x * lax.rsqrt(jnp.mean(x * x, axis=-1, keepdims=True) + EPS) * gain


def _norm_mod(x, gain, shift, scale):
    return _rms(x, gain) * (1.0 + scale) + shift


def _store_vt(vt_ref, head, v):
    dv = v.shape[1]
    vt_ref[head, 0:dv, :] = v.T.astype(BF16)
    vt_ref[head, dv:dv + V_PAD, :] = jnp.ones((V_PAD, v.shape[0]), BF16)


def _mod_kernel(cond_ref, w_ref, b_ref, o_ref):
    cnd = cond_ref[...]
    act = cnd * jax.nn.sigmoid(cnd)
    o_ref[...] = jnp.dot(act, w_ref[...], preferred_element_type=F32, precision=HIGHEST) + b_ref[...]


def _modulation(cond, w_mod, b_mod):
    depth, d, n = w_mod.shape
    tn = 1024
    rows = cond.shape[0]
    out = pl.pallas_call(
        _mod_kernel,
        grid=(depth, n // tn),
        in_specs=[
            pl.BlockSpec((rows, d), lambda l, j: (0, 0)),
            pl.BlockSpec((None, d, tn), lambda l, j: (l, 0, j)),
            pl.BlockSpec((None, 1, tn), lambda l, j: (l, 0, j)),
        ],
        out_specs=pl.BlockSpec((None, rows, tn), lambda l, j: (l, 0, j)),
        out_shape=jax.ShapeDtypeStruct((depth, rows, n), F32),
        compiler_params=_cparams("parallel", "parallel"),
        name="modulation",
    )(cond, w_mod, b_mod.reshape(depth, 1, n))
    return out.reshape(depth, rows, N_MOD, d)


def _rope_tables(lay, rot_dim):
    n_rows = lay.seq // GRID_W
    n_freq = rot_dim // 4
    inv_freq = ROPE_THETA ** (-jnp.arange(n_freq, dtype=F32) / n_freq)
    rows = jnp.repeat(jnp.arange(n_rows, dtype=F32), GRID_W)
    cols = jnp.tile(jnp.arange(GRID_W, dtype=F32), n_rows)
    ang = jnp.concatenate([rows[:, None] * inv_freq, cols[:, None] * inv_freq], -1)
    cs, sn = jnp.cos(ang), jnp.sin(ang)
    cos_f = jnp.concatenate([cs, cs], -1)
    sin_f = jnp.concatenate([-sn, sn], -1)
    cos_f = jnp.concatenate([jnp.ones((lay.ctx_len, rot_dim), F32), cos_f], 0)
    sin_f = jnp.concatenate([jnp.zeros((lay.ctx_len, rot_dim), F32), sin_f], 0)
    return cos_f, sin_f


def _even_front_kernel(x_ref, mod_ref, ng_ref, w_ref, qg_ref, kg_ref, cos_ref, sin_ref,
                       q_ref, k_ref, vt_ref, u_ref, ub_ref):
    h = _norm_mod(x_ref[...], ng_ref[...], mod_ref[0:1, :], mod_ref[1:2, :])
    p = jnp.dot(h.astype(BF16), w_ref[...], preferred_element_type=F32)
    cos, sin = cos_ref[...], sin_ref[...]
    hd = A_HEAD_DIM
    a_w, kv_w = A_HEADS * hd, A_KV_HEADS * hd

    def head(t, gain):
        t = _rms(t, gain)
        return t * cos + pltpu.roll(t, hd // 2, 1) * sin

    q_scale = hd ** -0.5 * LOG2E
    for i in range(A_HEADS):
        q_ref[:, i * hd:(i + 1) * hd] = (head(p[:, i * hd:(i + 1) * hd], qg_ref[...]) * q_scale).astype(BF16)
    for i in range(A_KV_HEADS):
        lo = a_w + i * hd
        k_ref[:, i * hd:(i + 1) * hd] = head(p[:, lo:lo + hd], kg_ref[...]).astype(BF16)
        _store_vt(vt_ref, i, p[:, lo + kv_w:lo + kv_w + hd])
    u = p[:, a_w + 2 * kv_w:]
    u_ref[...] = u.astype(BF16)
    for c in range(ROW_TILE // S5_CHUNK):
        ub_ref[c] = u[c * S5_CHUNK:(c + 1) * S5_CHUNK, :].T


def _even_front(lay, xs, mod, norm_g, w_in, qn_g, kn_g, cos_f, sin_f):
    d = xs.shape[1]
    hd = A_HEAD_DIM
    a_w, kv_w = A_HEADS * hd, A_KV_HEADS * hd
    b_w = w_in.shape[1] - a_w - 2 * kv_w
    pb = lambda t: (lay.batch_of(t), lay.pos_tile(t), 0)
    shp = lambda w: jax.ShapeDtypeStruct((lay.batch, lay.per_batch, w), BF16)
    cpt = ROW_TILE // S5_CHUNK
    return pl.pallas_call(
        _even_front_kernel,
        grid=(lay.n_tiles,),
        in_specs=[
            pl.BlockSpec((ROW_TILE, d), lambda t: (t, 0)),
            pl.BlockSpec((None, N_MOD, d), lambda t: (lay.mod_row(t), 0, 0)),
            _const_spec((1, d)),
            _const_spec(w_in.shape),
            _const_spec((1, hd)),
            _const_spec((1, hd)),
            pl.BlockSpec((ROW_TILE, hd), lambda t: (lay.pos_tile(t), 0)),
            pl.BlockSpec((ROW_TILE, hd), lambda t: (lay.pos_tile(t), 0)),
        ],
        out_specs=[
            pl.BlockSpec((None, ROW_TILE, a_w), pb),
            pl.BlockSpec((None, ROW_TILE, kv_w), pb),
            pl.BlockSpec((None, A_KV_HEADS, hd + V_PAD, ROW_TILE),
                         lambda t: (lay.batch_of(t), 0, 0, lay.pos_tile(t))),
            pl.BlockSpec((None, ROW_TILE, b_w), pb),
            pl.BlockSpec((cpt, b_w, S5_CHUNK), lambda t: (lay.chunk_tile(t), 0, 0)),
        ],
        out_shape=[shp(a_w), shp(kv_w),
                   jax.ShapeDtypeStruct((lay.batch, A_KV_HEADS, hd + V_PAD, lay.per_batch), BF16),
                   shp(b_w),
                   jax.ShapeDtypeStruct((lay.batch * lay.per_batch // S5_CHUNK, b_w, S5_CHUNK), F32)],
        compiler_params=_cparams("parallel"),
        name="even_front",
    )(xs, mod, norm_g.reshape(1, d), w_in, qn_g.reshape(1, hd), kn_g.reshape(1, hd), cos_f, sin_f)


def _attn_kernel(q_ref, k_ref, vt_ref, o_ref, s_sc, *, heads, group, dk, dv, ctx_len, kv_chunk, n_chunks):
    is_lat = pl.program_id(2) > 0
    n_cols = heads * group
    lat_chunks = [(0, ctx_len)] + [(ctx_len + c * kv_chunk, kv_chunk) for c in range(n_chunks)]

    def scores(col, start, size):
        hh = col // group
        k = k_ref[start:start + size, hh * dk:(hh + 1) * dk]
        q = q_ref[:, col * dk:(col + 1) * dk]
        s = lax.dot_general(k, q, (((1,), (1,)), ((), ())), preferred_element_type=F32)
        s_sc[col % 2, start:start + size, :] = s
        return jnp.max(s, axis=0, keepdims=True)

    def weighted(col, start, size, m):
        p = jnp.exp2(s_sc[col % 2, start:start + size, :] - m).astype(BF16)
        return jnp.dot(vt_ref[col // group, :, start:start + size], p, preferred_element_type=F32)

    def finish(col, acc):
        out = acc[0:dv, :] / acc[dv:dv + 1, :]
        o_ref[:, col * dv:(col + 1) * dv] = out.T.astype(o_ref.dtype)

    def run(chunks):
        m_cur = None
        for start, size in chunks:
            mx = scores(0, start, size)
            m_cur = mx if m_cur is None else jnp.maximum(m_cur, mx)
        for col in range(n_cols):
            m_next, acc = None, None
            for start, size in chunks:
                if col + 1 < n_cols:
                    mx = scores(col + 1, start, size)
                    m_next = mx if m_next is None else jnp.maximum(m_next, mx)
                pv = weighted(col, start, size, m_cur)
                acc = pv if acc is None else acc + pv
            finish(col, acc)
            m_cur = m_next

    @pl.when(is_lat)
    def _():
        run(lat_chunks)

    @pl.when(jnp.logical_not(is_lat))
    def _():
        run(lat_chunks[:1])


def _attention(lay, q, k, vt, *, kv_heads, heads_per_step, group, dk, dv, kv_chunk=512):
    b, p_len, _ = q.shape
    hb = heads_per_step
    assert lay.ctx_len == ROW_TILE and lay.seq % kv_chunk == 0 and kv_heads % hb == 0
    kern = functools.partial(_attn_kernel, heads=hb, group=group, dk=dk, dv=dv, ctx_len=lay.ctx_len,
                             kv_chunk=kv_chunk, n_chunks=lay.seq // kv_chunk)
    return pl.pallas_call(
        kern,
        grid=(b, kv_heads // hb, p_len // ROW_TILE),
        in_specs=[
            pl.BlockSpec((None, ROW_TILE, hb * group * dk), lambda bi, h, t: (bi, t, h)),
            pl.BlockSpec((None, p_len, hb * dk), lambda bi, h, t: (bi, 0, h)),
            pl.BlockSpec((None, hb, dv + V_PAD, p_len), lambda bi, h, t: (bi, h, 0, 0)),
        ],
        out_specs=pl.BlockSpec((None, ROW_TILE, hb * group * dv), lambda bi, h, t: (bi, t, h)),
        out_shape=jax.ShapeDtypeStruct((b, p_len, kv_heads * group * dv), BF16),
        scratch_shapes=[pltpu.VMEM((2, p_len, ROW_TILE), F32)],
        compiler_params=_cparams("parallel", "parallel", "arbitrary"),
        name="attention",
    )(q, k, vt)


def _s5_operators(lam_re, lam_im, log_dt, b_re, b_im, c_re, c_im, chunk):
    g, p_dim, cg = b_re.shape
    lam = lax.complex(jnp.minimum(lam_re.astype(F32), -1e-4), lam_im.astype(F32))
    dt = jnp.exp(log_dt.astype(F32))[:, None]
    lam_dt = lam * dt
    lam_bar = jnp.exp(lam_dt)
    b_bar = ((lam_bar - 1.0) / lam)[..., None] * lax.complex(b_re.astype(F32), b_im.astype(F32))
    cmat = lax.complex(c_re.astype(F32), c_im.astype(F32))
    steps = jnp.arange(chunk + 1, dtype=F32)
    pw = jnp.exp(lam_dt[None] * steps[:, None, None].astype(jnp.complex64))
    kern = jnp.real(jnp.einsum('gcp,tgp,gpi->gict', cmat, pw[:chunk], b_bar))
    st = jnp.einsum('tgp,gpc->gctp', pw[chunk - 1 - jnp.arange(chunk)], b_bar).reshape(g, cg * chunk, p_dim)
    so = jnp.einsum('gcp,tgp->gpct', cmat, pw[1:chunk + 1]).reshape(g, p_dim, cg * chunk)
    decay = pw[chunk]
    return (kern, jnp.real(st), jnp.imag(st), jnp.real(so), -jnp.imag(so), jnp.real(decay), jnp.imag(decay))


def _flip_time(m, chunk, cg, ax):
    shp = m.shape
    m = m.reshape(shp[:ax] + (cg, chunk) + shp[ax + 1:])
    return jnp.flip(m, ax + 1).reshape(shp)


def _group_rows(ub_ref, gi):
    cg = S5_GROUP
    return jnp.concatenate([ub_ref[:, gi * cg + c, :] for c in range(cg)], axis=1).astype(BF16)


def _s5_state_in_kernel(ub_ref, w_ref, o0, o1, o2, o3):
    for gi in range(w_ref.shape[0]):
        x = _group_rows(ub_ref, gi)
        for j, o in enumerate((o0, o1, o2, o3)):
            o[gi] = jnp.dot(x, w_ref[gi, j], preferred_element_type=F32)


def _s5_carry_kernel(sfr, sfi, sbr, sbi, lfr, lfi, lbr, lbi, hfr, hfi, hbr, hbi, *, n_chunks, ctx_chunks, rows):
    a_fr, a_fi, a_br, a_bi = lfr[...], lfi[...], lbr[...], lbi[...]

    def rows_at(kk):
        return pl.ds(kk, rows, stride=n_chunks)

    def body(i, carry):
        f_r, f_i, b_r, b_i = carry
        kf = i
        kb = jnp.where(i < ctx_chunks, ctx_chunks - 1 - i, n_chunks - 1 - (i - ctx_chunks))
        hfr[rows_at(kf), :] = f_r
        hfi[rows_at(kf), :] = f_i
        hbr[rows_at(kb), :] = b_r
        hbi[rows_at(kb), :] = b_i
        nf_r = a_fr * f_r - a_fi * f_i + sfr[rows_at(kf), :]
        nf_i = a_fr * f_i + a_fi * f_r + sfi[rows_at(kf), :]
        nb_r = a_br * b_r - a_bi * b_i + sbr[rows_at(kb), :]
        nb_i = a_br * b_i + a_bi * b_r + sbi[rows_at(kb), :]
        return nf_r, nf_i, nb_r, nb_i

    z = jnp.zeros(a_fr.shape, F32)
    lax.fori_loop(0, n_chunks, body, (z, z, z, z))


def _s5_main_kernel(ub_ref, kf_ref, kb_ref, wo_ref, h0, h1, h2, h3, y_ref, m_ref):
    t, cg = S5_CHUNK, S5_GROUP
    row = lax.broadcasted_iota(jnp.int32, (t, t), 0)
    col = lax.broadcasted_iota(jnp.int32, (t, t), 1)
    causal = col >= row

    def toeplitz(vec):
        return pltpu.roll(jnp.broadcast_to(vec, (t, t)), 0, 1, stride=1, stride_axis=0)

    def build(ci, carry):
        kf, kb = kf_ref[ci], kb_ref[ci]
        r0 = pl.multiple_of(ci * t, t)
        for co in range(cg):
            blk = jnp.where(causal, toeplitz(kf[co:co + 1, :]), toeplitz(kb[co:co + 1, :]))
            m_ref[pl.ds(r0, t), co * t:(co + 1) * t] = blk.astype(BF16)
        return carry

    lax.fori_loop(0, cg, build, 0)
    acc = jnp.dot(_group_rows(ub_ref, 0), m_ref[...], preferred_element_type=F32)
    for j, h in enumerate((h0, h1, h2, h3)):
        acc += jnp.dot(h[...].astype(BF16), wo_ref[j], preferred_element_type=F32)
    for co in range(cg):
        y_ref[:, co, :] = acc[:, co * t:(co + 1) * t]


def _s5_scan(lay, ub, lam_re, lam_im, log_dt, b_re, b_im, c_re, c_im):
    rows, width, t = ub.shape
    cg, p_dim = S5_GROUP, S5_STATE
    g = width // cg
    bsz = lay.batch
    n_chunks = rows // bsz
    ctx_chunks = lay.ctx_len // t
    tc = t * cg
    fwd = _s5_operators(lam_re[0], lam_im[0], log_dt[0], b_re[0], b_im[0], c_re[0], c_im[0], t)
    bwd = _s5_operators(lam_re[1], lam_im[1], log_dt[1], b_re[1], b_im[1], c_re[1], c_im[1], t)
    kf = fwd[0].at[..., 0].add(bwd[0][..., 0])
    kb = jnp.roll(jnp.flip(bwd[0], -1), 1, axis=-1)
    w_in = jnp.stack([fwd[1], fwd[2], _flip_time(bwd[1], t, cg, 1), _flip_time(bwd[2], t, cg, 1)],
                     axis=1).astype(BF16)
    w_out = jnp.stack([fwd[3], fwd[4], _flip_time(bwd[3], t, cg, 2), _flip_time(bwd[4], t, cg, 2)],
                      axis=1).astype(BF16)

    gb = 4
    state_shape = jax.ShapeDtypeStruct((g, rows, p_dim), F32)
    st_spec = pl.BlockSpec((gb, rows, p_dim), lambda i: (i, 0, 0))
    s_in = pl.pallas_call(
        _s5_state_in_kernel,
        grid=(g // gb,),
        in_specs=[pl.BlockSpec((rows, gb * cg, t), lambda i: (0, i, 0)),
                  pl.BlockSpec((gb, 4, tc, p_dim), lambda i: (i, 0, 0, 0))],
        out_specs=[st_spec] * 4,
        out_shape=[state_shape] * 4,
        compiler_params=_cparams("parallel"),
        name="s5_state_in",
    )(ub, w_in)

    gb_c = 8
    crow = gb_c * bsz
    flat = lambda a: a.reshape(g * rows, p_dim)
    dec = [jnp.repeat(a, bsz, axis=0) for a in (fwd[5], fwd[6], bwd[5], bwd[6])]
    carry_spec = pl.BlockSpec((crow * n_chunks, p_dim), lambda i: (i, 0))
    h_in = pl.pallas_call(
        functools.partial(_s5_carry_kernel, n_chunks=n_chunks, ctx_chunks=ctx_chunks, rows=crow),
        grid=(g // gb_c,),
        in_specs=[carry_spec] * 4 + [pl.BlockSpec((crow, p_dim), lambda i: (i, 0))] * 4,
        out_specs=[carry_spec] * 4,
        out_shape=[jax.ShapeDtypeStruct((g * rows, p_dim), F32)] * 4,
        compiler_params=_cparams("parallel"),
        name="s5_carry",
    )(*[flat(a) for a in s_in], *dec)
    h_in = [a.reshape(g, rows, p_dim) for a in h_in]

    h_spec = pl.BlockSpec((None, rows, p_dim), lambda i: (i, 0, 0))
    return pl.pallas_call(
        _s5_main_kernel,
        grid=(g,),
        in_specs=[pl.BlockSpec((rows, cg, t), lambda i: (0, i, 0)),
                  pl.BlockSpec((None, cg, cg, t), lambda i: (i, 0, 0, 0)),
                  pl.BlockSpec((None, cg, cg, t), lambda i: (i, 0, 0, 0)),
                  pl.BlockSpec((None, 4, p_dim, tc), lambda i: (i, 0, 0, 0))] + [h_spec] * 4,
        out_specs=pl.BlockSpec((rows, cg, t), lambda i: (0, i, 0)),
        out_shape=jax.ShapeDtypeStruct((rows, width, t), F32),
        scratch_shapes=[pltpu.VMEM((tc, tc), BF16)],
        compiler_params=_cparams("parallel"),
        name="s5_main",
    )(ub, kf, kb, w_out, *h_in)


def _gelu_tanh(x):
    return 0.5 * x * (1.0 + jnp.tanh(math.sqrt(2.0 / math.pi) * (x + 0.044715 * (x * x * x))))


def _even_back_kernel(a_ref, yb_ref, u_ref, x_ref, mod_ref, d_ref, wg_ref, wo_ref, ng_ref, xo_ref, h_ref):
    y_ssm = jnp.concatenate([yb_ref[c].T for c in range(yb_ref.shape[0])], axis=0)
    y = d_ref[...] * u_ref[...].astype(F32) + y_ssm
    gl = _gelu_tanh(y)
    gate = jnp.dot(gl.astype(BF16), wg_ref[...], preferred_element_type=F32)
    s = gl * jax.nn.sigmoid(gate)
    a_w = a_ref.shape[1]
    o = jnp.dot(a_ref[...], wo_ref[0:a_w, :], preferred_element_type=F32)
    o += jnp.dot(s.astype(BF16), wo_ref[a_w:, :], preferred_element_type=F32)
    x_new = x_ref[...] + mod_ref[2:3, :] * o
    xo_ref[...] = x_new
    h_ref[...] = _norm_mod(x_new, ng_ref[...], mod_ref[3:4, :], mod_ref[4:5, :]).astype(BF16)


def _even_back(lay, attn, yb, u, xs, mod, d_skip, w_glu, w_out, norm_g):
    n, d = xs.shape
    a_w, b_w = attn.shape[2], u.shape[2]
    pb = lambda t: (lay.batch_of(t), lay.pos_tile(t), 0)
    cpt = ROW_TILE // S5_CHUNK
    return pl.pallas_call(
        _even_back_kernel,
        grid=(lay.n_tiles,),
        in_specs=[
            pl.BlockSpec((None, ROW_TILE, a_w), pb),
            pl.BlockSpec((cpt, b_w, S5_CHUNK), lambda t: (lay.chunk_tile(t), 0, 0)),
            pl.BlockSpec((None, ROW_TILE, b_w), pb),
            pl.BlockSpec((ROW_TILE, d), lambda t: (t, 0)),
            pl.BlockSpec((None, N_MOD, d), lambda t: (lay.mod_row(t), 0, 0)),
            _const_spec((1, b_w)),
            _const_spec(w_glu.shape),
            _const_spec(w_out.shape),
            _const_spec((1, d)),
        ],
        out_specs=[pl.BlockSpec((ROW_TILE, d), lambda t: (t, 0)),
                   pl.BlockSpec((ROW_TILE, d), lambda t: (t, 0))],
        out_shape=[jax.ShapeDtypeStruct((n, d), F32), jax.ShapeDtypeStruct((n, d), BF16)],
        compiler_params=_cparams("parallel"),
        name="even_back",
    )(attn, yb, u, xs, mod, d_skip.reshape(1, b_w), w_glu, w_out, norm_g.reshape(1, d))


def _ffn_kernel(h_ref, x_ref, mod_ref, w1_ref, w3_ref, w2_ref, o_ref, acc_ref):
    j = pl.program_id(1)

    @pl.when(j == 0)
    def _():
        acc_ref[...] = jnp.zeros(acc_ref.shape, F32)

    h = h_ref[...]
    a = jnp.dot(h, w1_ref[...], preferred_element_type=F32)
    b = jnp.dot(h, w3_ref[...], preferred_element_type=F32)
    z = (a * jax.nn.sigmoid(a) * b).astype(BF16)
    acc_ref[...] += jnp.dot(z, w2_ref[...], preferred_element_type=F32)

    @pl.when(j == pl.num_programs(1) - 1)
    def _():
        o_ref[...] = x_ref[...] + mod_ref[5:6, :] * acc_ref[...]


def _ffn(lay, h, xs, mod, w1, w3, w2):
    n, d = xs.shape
    f = w1.shape[1]
    tm, tf = FFN_TILE_M, FFN_TILE_F
    assert n % tm == 0 and f % tf == 0 and lay.n_ctx % tm == 0 and lay.seq % tm == 0
    return pl.pallas_call(
        _ffn_kernel,
        grid=(n // tm, f // tf),
        in_specs=[
            pl.BlockSpec((tm, d), lambda i, j: (i, 0)),
            pl.BlockSpec((tm, d), lambda i, j: (i, 0)),
            pl.BlockSpec((None, N_MOD, d), lambda i, j: (lay.mod_row(i, tm), 0, 0)),
            pl.BlockSpec((d, tf), lambda i, j: (0, j)),
            pl.BlockSpec((d, tf), lambda i, j: (0, j)),
            pl.BlockSpec((tf, d), lambda i, j: (j, 0)),
        ],
        out_specs=pl.BlockSpec((tm, d), lambda i, j: (i, 0)),
        out_shape=jax.ShapeDtypeStruct((n, d), F32),
        scratch_shapes=[pltpu.VMEM((tm, d), F32)],
        compiler_params=_cparams("parallel", "arbitrary"),
        name="dense_swiglu",
    )(h, xs, mod, w1, w3, w2)


def _odd_front_a_kernel(x_ref, mod_ref, ng_ref, w_ref, qg_ref, kvg_ref, cq_ref, ckv_ref, kpe_ref):
    h = _norm_mod(x_ref[...], ng_ref[...], mod_ref[0:1, :], mod_ref[1:2, :])
    p = jnp.dot(h.astype(BF16), w_ref[...], preferred_element_type=F32)
    cq_ref[...] = _rms(p[:, :Q_LORA], qg_ref[...]).astype(BF16)
    ckv_ref[...] = _rms(p[:, Q_LORA:Q_LORA + KV_LORA], kvg_ref[...]).astype(BF16)
    kpe_ref[...] = p[:, Q_LORA + KV_LORA:]


def _swap_halves(t, width):
    half = width // 2
    if width == LANE:
        return pltpu.roll(t, half, 1)
    lane = lax.broadcasted_iota(jnp.int32, t.shape, 1)
    return jnp.where(lane % width < half, pltpu.roll(t, LANE - half, 1), pltpu.roll(t, half, 1))


def _segment_rms(t, width, seg_ones):
    ss = jnp.dot(t * t, seg_ones, preferred_element_type=F32, precision=HIGHEST)
    return t * lax.rsqrt(ss * (1.0 / width) + EPS)


def _odd_front_b_kernel(cq_ref, ckv_ref, kpe_ref, wq_ref, wkv_ref, qg_ref, kg_ref, cos_ref, sin_ref, seg_ref,
                        q_ref, k_ref, vt_ref):
    q = jnp.dot(cq_ref[...], wq_ref[...], preferred_element_type=F32)
    kv = jnp.dot(ckv_ref[...], wkv_ref[...], preferred_element_type=F32)
    cos, sin, seg = cos_ref[...], sin_ref[...], seg_ref[...]
    qg_n, qg_r = qg_ref[:, :LANE], qg_ref[:, LANE:]
    kg_n, kg_r = kg_ref[:, :LANE], kg_ref[:, LANE:]
    scale = (C_NOPE + C_ROPE) ** -0.5 * LOG2E
    dkp = 2 * LANE
    lane = lax.broadcasted_iota(jnp.int32, (q.shape[0], LANE), 1)

    def rope(t):
        return t * cos + _swap_halves(t, C_ROPE) * sin

    kpe = rope(_segment_rms(kpe_ref[...], C_ROPE, seg) * kg_r)
    kpe_even = jnp.where(lane < C_ROPE, kpe, 0.0).astype(BF16)
    kpe_odd = jnp.where(lane < C_ROPE, 0.0, pltpu.roll(kpe, C_ROPE, 1)).astype(BF16)
    rope_base = C_HEADS * C_NOPE
    for pair in range(C_HEADS // 2):
        qp = q[:, rope_base + pair * LANE:rope_base + (pair + 1) * LANE]
        qp = (rope(_segment_rms(qp, C_ROPE, seg) * qg_r) * scale).astype(BF16)
        for hh in (2 * pair, 2 * pair + 1):
            qn = _rms(q[:, hh * C_NOPE:(hh + 1) * C_NOPE], qg_n) * scale
            q_ref[:, hh * dkp:hh * dkp + LANE] = qn.astype(BF16)
            q_ref[:, hh * dkp + LANE:(hh + 1) * dkp] = qp
            kn = _rms(kv[:, hh * 2 * LANE:hh * 2 * LANE + C_NOPE], kg_n)
            k_ref[:, hh * dkp:hh * dkp + LANE] = kn.astype(BF16)
            k_ref[:, hh * dkp + LANE:(hh + 1) * dkp] = kpe_even if hh % 2 == 0 else kpe_odd
            _store_vt(vt_ref, hh, kv[:, hh * 2 * LANE + C_NOPE:(hh + 1) * 2 * LANE])


def _odd_front(lay, xs, mod, norm_g, w_in, q_lora_g, kv_lora_g, w_uq, w_ukv, qn_g, kn_g, cos_f, sin_f):
    n, d = xs.shape
    assert C_NOPE == LANE and C_V == LANE and 2 * C_ROPE == LANE
    cq, ckv, kpe = pl.pallas_call(
        _odd_front_a_kernel,
        grid=(lay.n_tiles,),
        in_specs=[
            pl.BlockSpec((ROW_TILE, d), lambda t: (t, 0)),
            pl.BlockSpec((None, N_MOD, d), lambda t: (lay.mod_row(t), 0, 0)),
            _const_spec((1, d)),
            _const_spec(w_in.shape),
            _const_spec((1, Q_LORA)),
            _const_spec((1, KV_LORA)),
        ],
        out_specs=[pl.BlockSpec((ROW_TILE, Q_LORA), lambda t: (t, 0)),
                   pl.BlockSpec((ROW_TILE, KV_LORA), lambda t: (t, 0)),
                   pl.BlockSpec((ROW_TILE, LANE), lambda t: (t, 0))],
        out_shape=[jax.ShapeDtypeStruct((n, Q_LORA), BF16), jax.ShapeDtypeStruct((n, KV_LORA), BF16),
                   jax.ShapeDtypeStruct((n, LANE), F32)],
        compiler_params=_cparams("parallel"),
        name="odd_front_a",
    )(xs, mod, norm_g.reshape(1, d), w_in, q_lora_g.reshape(1, Q_LORA), kv_lora_g.reshape(1, KV_LORA))

    seg = (jnp.arange(LANE)[:, None] // C_ROPE == jnp.arange(LANE)[None, :] // C_ROPE).astype(F32)
    pair_gain = lambda g: jnp.concatenate([g[:C_NOPE], g[C_NOPE:], g[C_NOPE:]]).astype(F32).reshape(1, 2 * LANE)
    pb = lambda t: (lay.batch_of(t), lay.pos_tile(t), 0)
    shp = lambda w: jax.ShapeDtypeStruct((lay.batch, lay.per_batch, w), BF16)
    dkp = 2 * LANE
    return pl.pallas_call(
        _odd_front_b_kernel,
        grid=(lay.n_tiles,),
        in_specs=[
            pl.BlockSpec((ROW_TILE, Q_LORA), lambda t: (t, 0)),
            pl.BlockSpec((ROW_TILE, KV_LORA), lambda t: (t, 0)),
            pl.BlockSpec((ROW_TILE, LANE), lambda t: (t, 0)),
            _const_spec(w_uq.shape),
            _const_spec(w_ukv.shape),
            _const_spec((1, 2 * LANE)),
            _const_spec((1, 2 * LANE)),
            pl.BlockSpec((ROW_TILE, LANE), lambda t: (lay.pos_tile(t), 0)),
            pl.BlockSpec((ROW_TILE, LANE), lambda t: (lay.pos_tile(t), 0)),
            _const_spec((LANE, LANE)),
        ],
        out_specs=[pl.BlockSpec((None, ROW_TILE, C_HEADS * dkp), pb),
                   pl.BlockSpec((None, ROW_TILE, C_HEADS * dkp), pb),
                   pl.BlockSpec((None, C_HEADS, C_V + V_PAD, ROW_TILE),
                                lambda t: (lay.batch_of(t), 0, 0, lay.pos_tile(t)))],
        out_shape=[shp(C_HEADS * dkp), shp(C_HEADS * dkp),
                   jax.ShapeDtypeStruct((lay.batch, C_HEADS, C_V + V_PAD, lay.per_batch), BF16)],
        compiler_params=_cparams("parallel"),
        name="odd_front_b",
    )(cq, ckv, kpe, w_uq, w_ukv, pair_gain(qn_g), pair_gain(kn_g), cos_f, sin_f, seg)


def _odd_back_kernel(a_ref, x_ref, mod_ref, wo_ref, ng_ref, wr_ref, xo_ref, h_ref, r_ref):
    o = jnp.dot(a_ref[...], wo_ref[...], preferred_element_type=F32)
    x_new = x_ref[...] + mod_ref[2:3, :] * o
    xo_ref[...] = x_new
    h = _norm_mod(x_new, ng_ref[...], mod_ref[3:4, :], mod_ref[4:5, :])
    h_ref[...] = h.astype(BF16)
    logits = jnp.dot(h, wr_ref[...], preferred_element_type=F32, precision=HIGHEST)
    lane = lax.broadcasted_iota(jnp.int32, logits.shape, 1).astype(F32)
    neg = jnp.float32(-jnp.inf)
    lg = jnp.where(lane < N_EXPERTS, logits, neg)
    m1 = jnp.max(lg, axis=-1, keepdims=True)
    i1 = jnp.min(jnp.where(lg == m1, lane, float(LANE)), axis=-1, keepdims=True)
    lg2 = jnp.where(lane == i1, neg, lg)
    m2 = jnp.max(lg2, axis=-1, keepdims=True)
    i2 = jnp.min(jnp.where(lg2 == m2, lane, float(LANE)), axis=-1, keepdims=True)
    e2 = jnp.exp(m2 - m1)
    g1 = 1.0 / (1.0 + e2)
    g2 = e2 / (1.0 + e2)
    r = jnp.where(lane == 0.0, i1, 0.0)
    r = jnp.where(lane == 1.0, i2, r)
    r = jnp.where(lane == 2.0, g1, r)
    r_ref[...] = jnp.where(lane == 3.0, g2, r)


def _odd_back(lay, attn, xs, mod, w_o, norm_g, w_router):
    n, d = xs.shape
    a_w = attn.shape[2]
    wr = jnp.zeros((d, LANE), F32).at[:, :N_EXPERTS].set(w_router.astype(F32))
    pb = lambda t: (lay.batch_of(t), lay.pos_tile(t), 0)
    row = lambda w: pl.BlockSpec((ROW_TILE, w), lambda t: (t, 0))
    return pl.pallas_call(
        _odd_back_kernel,
        grid=(lay.n_tiles,),
        in_specs=[
            pl.BlockSpec((None, ROW_TILE, a_w), pb),
            row(d),
            pl.BlockSpec((None, N_MOD, d), lambda t: (lay.mod_row(t), 0, 0)),
            _const_spec(w_o.shape),
            _const_spec((1, d)),
            _const_spec((d, LANE)),
        ],
        out_specs=[row(d), row(d), row(LANE)],
        out_shape=[jax.ShapeDtypeStruct((n, d), F32), jax.ShapeDtypeStruct((n, d), BF16),
                   jax.ShapeDtypeStruct((n, LANE), F32)],
        compiler_params=_cparams("parallel"),
        name="odd_back",
    )(attn, xs, mod, w_o, norm_g.reshape(1, d), wr)


def _moe_kernel(te_ref, tv_ref, h_ref, g_ref, w1_ref, w3_ref, w2_ref, o_ref, acc_ref):
    r, j = pl.program_id(0), pl.program_id(1)
    last = pl.num_programs(1) - 1
    valid = tv_ref[r] > 0

    @pl.when(j == 0)
    def _():
        acc_ref[...] = jnp.zeros(acc_ref.shape, F32)

    @pl.when(valid)
    def _():
        h = h_ref[...]
        a = jnp.dot(h, w1_ref[...], preferred_element_type=F32)
        b = jnp.dot(h, w3_ref[...], preferred_element_type=F32)
        z = (a * jax.nn.sigmoid(a) * b).astype(BF16)
        acc_ref[...] += jnp.dot(z, w2_ref[...], preferred_element_type=F32)

    @pl.when(j == last)
    def _():
        o_ref[...] = g_ref[...] * acc_ref[...]


def _moe_ffn(h_sorted, gate_sorted, tile_expert, tile_valid, w1, w3, w2):
    s_pad, d = h_sorted.shape
    f = w1.shape[2]
    tm, tf = MOE_TILE, 512
    n_tiles = s_pad // tm
    jf = lambda r, j, tv: jnp.where(tv[r] > 0, j, f // tf - 1)
    grid_spec = pltpu.PrefetchScalarGridSpec(
        num_scalar_prefetch=2,
        grid=(n_tiles, f // tf),
        in_specs=[
            pl.BlockSpec((tm, d), lambda r, j, te, tv: (r, 0)),
            pl.BlockSpec((tm, 1), lambda r, j, te, tv: (r, 0)),
            pl.BlockSpec((None, d, tf), lambda r, j, te, tv: (te[r], 0, jf(r, j, tv))),
            pl.BlockSpec((None, d, tf), lambda r, j, te, tv: (te[r], 0, jf(r, j, tv))),
            pl.BlockSpec((None, tf, d), lambda r, j, te, tv: (te[r], jf(r, j, tv), 0)),
        ],
        out_specs=pl.BlockSpec((tm, d), lambda r, j, te, tv: (r, 0)),
        scratch_shapes=[pltpu.VMEM((tm, d), F32)],
    )
    return pl.pallas_call(
        _moe_kernel,
        grid_spec=grid_spec,
        out_shape=jax.ShapeDtypeStruct((s_pad, d), F32),
        compiler_params=_cparams("arbitrary", "arbitrary"),
        name="moe_experts",
    )(tile_expert, tile_valid, h_sorted, gate_sorted, w1, w3, w2)


def _combine_kernel(x_ref, mod_ref, y0_ref, y1_ref, o_ref):
    o_ref[...] = x_ref[...] + mod_ref[5:6, :] * (y0_ref[...] + y1_ref[...])


def _moe_combine(lay, xs, mod, y0, y1):
    n, d = xs.shape
    row = pl.BlockSpec((ROW_TILE, d), lambda t: (t, 0))
    return pl.pallas_call(
        _combine_kernel,
        grid=(lay.n_tiles,),
        in_specs=[row, pl.BlockSpec((None, N_MOD, d), lambda t: (lay.mod_row(t), 0, 0)), row, row],
        out_specs=row,
        out_shape=jax.ShapeDtypeStruct((n, d), F32),
        compiler_params=_cparams("parallel"),
        name="moe_combine",
    )(xs, mod, y0, y1)


def _moe(lay, h, route, xs, mod, w1, w3, w2):
    n, d = xs.shape
    tm = MOE_TILE
    n_assign = 2 * n
    n_tiles = n_assign // tm + N_EXPERTS
    s_pad = n_tiles * tm
    expert = route[:, 0:2].astype(jnp.int32).reshape(n_assign)
    gate = route[:, 2:4].reshape(n_assign)
    onehot = (expert[:, None] == jnp.arange(N_EXPERTS)[None, :]).astype(jnp.int32)
    rank = jnp.take_along_axis(jnp.cumsum(onehot, axis=0) - onehot, expert[:, None], axis=1)[:, 0]
    count = jnp.sum(onehot, axis=0)
    padded = (count + tm - 1) // tm * tm
    end = jnp.cumsum(padded)
    start = end - padded
    slot = start[expert] + rank
    token = jnp.arange(n_assign, dtype=jnp.int32) // 2
    tok_of_slot = jnp.zeros((s_pad,), jnp.int32).at[slot].set(token)
    gate_of_slot = jnp.zeros((s_pad,), F32).at[slot].set(gate)
    tile_start = jnp.arange(n_tiles, dtype=jnp.int32) * tm
    tile_expert = jnp.minimum(jnp.sum((tile_start[:, None] >= end[None, :]).astype(jnp.int32), axis=1),
                              N_EXPERTS - 1).astype(jnp.int32)
    tile_valid = (tile_start < end[-1]).astype(jnp.int32)
    h_sorted = jnp.take(h, tok_of_slot, axis=0)
    y = _moe_ffn(h_sorted, gate_of_slot[:, None], tile_expert, tile_valid, w1, w3, w2)
    slots = slot.reshape(n, 2)
    y0 = jnp.take(y, slots[:, 0], axis=0)
    y1 = jnp.take(y, slots[:, 1], axis=0)
    return _moe_combine(lay, xs, mod, y0, y1)


def kernel(x, c, ctx, c_ctx, w_mod, b_mod, norm1_g, norm2_g, ev_w_in, ev_qn_g, ev_kn_g, s5_lam_re, s5_lam_im, s5_log_dt, s5_b_re, s5_b_im, s5_c_re, s5_c_im, s5_d, ev_w_glu, ev_w_out, ffn_w1, ffn_w3, ffn_w2, od_w_in, od_q_lora_g, od_kv_lora_g, od_w_uq, od_w_ukv, od_qn_g, od_kn_g, od_w_o, moe_router, moe_w1, moe_w3, moe_w2):
    bsz, seq, d = x.shape
    ctx_len = ctx.shape[1]
    depth = w_mod.shape[0]
    lay = _Layout(bsz, ctx_len, seq)

    cond = jnp.zeros((8, d), F32).at[:bsz].set(c).at[bsz].set(c_ctx)
    mods = _modulation(cond, w_mod, b_mod)

    cos_a, sin_a = _rope_tables(lay, A_HEAD_DIM)
    cos_c, sin_c = _rope_tables(lay, C_ROPE)
    cos_c = jnp.concatenate([cos_c, cos_c], -1)
    sin_c = jnp.concatenate([sin_c, sin_c], -1)

    nope_cols = (jnp.arange(C_HEADS)[:, None] * (C_NOPE + C_ROPE) + jnp.arange(C_NOPE)[None, :]).reshape(-1)
    rope_cols = (jnp.arange(C_HEADS)[:, None] * (C_NOPE + C_ROPE) + C_NOPE + jnp.arange(C_ROPE)[None, :]).reshape(-1)
    uq_cols = jnp.concatenate([nope_cols, rope_cols])

    xs = jnp.concatenate([ctx.reshape(bsz * ctx_len, d), x.reshape(bsz * seq, d)], axis=0)
    for i in range(depth):
        j = i // 2
        mod = mods[i]
        if i % 2 == 0:
            q, k, vt, u, ub = _even_front(lay, xs, mod, norm1_g[i], ev_w_in[j].astype(BF16), ev_qn_g[j],
                                          ev_kn_g[j], cos_a, sin_a)
            attn = _attention(lay, q, k, vt, kv_heads=A_KV_HEADS, heads_per_step=1,
                              group=A_HEADS // A_KV_HEADS, dk=A_HEAD_DIM, dv=A_HEAD_DIM)
            yb = _s5_scan(lay, ub, s5_lam_re[j], s5_lam_im[j], s5_log_dt[j], s5_b_re[j], s5_b_im[j],
                          s5_c_re[j], s5_c_im[j])
            xs, h = _even_back(lay, attn, yb, u, xs, mod, s5_d[j], ev_w_glu[j].astype(BF16),
                               ev_w_out[j].astype(BF16), norm2_g[i])
            xs = _ffn(lay, h, xs, mod, ffn_w1[j].astype(BF16), ffn_w3[j].astype(BF16), ffn_w2[j].astype(BF16))
        else:
            w_in = jnp.pad(od_w_in[j], ((0, 0), (0, LANE - C_ROPE))).astype(BF16)
            q, k, vt = _odd_front(lay, xs, mod, norm1_g[i], w_in, od_q_lora_g[j], od_kv_lora_g[j],
                                  od_w_uq[j][:, uq_cols].astype(BF16), od_w_ukv[j].astype(BF16),
                                  od_qn_g[j], od_kn_g[j], cos_c, sin_c)
            attn = _attention(lay, q, k, vt, kv_heads=C_HEADS, heads_per_step=4, group=1, dk=2 * LANE, dv=C_V)
            xs, h, route = _odd_back(lay, attn, xs, mod, od_w_o[j].astype(BF16), norm2_g[i], moe_router[j])
            xs = _moe(lay, h, route, xs, mod, moe_w1[j].astype(BF16), moe_w3[j].astype(BF16),
                      moe_w2[j].astype(BF16))
    return xs[bsz * ctx_len:].reshape(bsz, seq, d)
```

```python
import functools
import math

import jax
import jax.numpy as jnp
from jax import lax
from jax.experimental import pallas as pl
from jax.experimental.pallas import tpu as pltpu

F32 = jnp.float32
BF16 = jnp.bfloat16
HIGHEST = lax.Precision.HIGHEST

EPS = 1e-6
ROPE_THETA = 10000.0
GRID_W = 64
N_MOD = 6
ROW_TILE = 256
LANE = 128
VMEM_LIMIT = 56 * 1024 * 1024
LOG2E = 1.4426950408889634
V_PAD = 16

A_HEADS, A_KV_HEADS, A_HEAD_DIM = 8, 2, 128
S5_GROUP, S5_STATE = 16, 64
S5_CHUNK = LANE
C_HEADS, C_NOPE, C_ROPE, C_V = 16, 128, 64, 128
Q_LORA, KV_LORA = 1536, 512
N_EXPERTS = 8
MOE_TILE, MOE_TILE_F = 512, 1024
FFN_TILE_M, FFN_TILE_F = 512, 512


def _cparams(*sem):
    return pltpu.CompilerParams(dimension_semantics=sem, vmem_limit_bytes=VMEM_LIMIT)


def _const_spec(shape):
    nd = len(shape)
    return pl.BlockSpec(shape, lambda *_: (0,) * nd)


class _Layout:
    def __init__(self, batch, ctx_len, seq):
        assert ctx_len == ROW_TILE and seq % ROW_TILE == 0
        self.batch, self.ctx_len, self.seq = batch, ctx_len, seq
        self.n_ctx = batch * ctx_len
        self.n_tok = batch * (ctx_len + seq)
        self.per_batch = ctx_len + seq
        self.ctx_tiles = self.n_ctx // ROW_TILE
        self.lat_tiles_per_batch = seq // ROW_TILE
        self.n_tiles = self.n_tok // ROW_TILE

    def mod_row(self, t, tile=ROW_TILE):
        ctx_tiles = self.n_ctx // tile
        per_b = self.seq // tile
        return jnp.where(t < ctx_tiles, self.batch, (t - ctx_tiles) // per_b)

    def batch_of(self, t):
        return jnp.where(t < self.ctx_tiles, t, (t - self.ctx_tiles) // self.lat_tiles_per_batch)

    def pos_tile(self, t):
        return jnp.where(t < self.ctx_tiles, 0, 1 + (t - self.ctx_tiles) % self.lat_tiles_per_batch)

    def chunk_tile(self, t):
        return self.batch_of(t) * (self.per_batch // ROW_TILE) + self.pos_tile(t)


def _rms(x, gain):
    return x * lax.rsqrt(jnp.mean(x * x, axis=-1, keepdims=True) + EPS) * gain


def _norm_mod(x, gain, shift, scale):
    return _rms(x, gain) * (1.0 + scale) + shift


def _store_vt(vt_ref, head, v):
    dv = v.shape[1]
    vt_ref[head, 0:dv, :] = v.T.astype(BF16)
    vt_ref[head, dv:dv + V_PAD, :] = jnp.ones((V_PAD, v.shape[0]), BF16)


def _mod_kernel(cond_ref, w_ref, b_ref, o_ref):
    cnd = cond_ref[...]
    act = cnd * jax.nn.sigmoid(cnd)
    o_ref[...] = jnp.dot(act, w_ref[...], preferred_element_type=F32, precision=HIGHEST) + b_ref[...]


def _modulation(cond, w_mod, b_mod):
    depth, d, n = w_mod.shape
    tn = 1024
    rows = cond.shape[0]
    out = pl.pallas_call(
        _mod_kernel,
        grid=(depth, n // tn),
        in_specs=[
            pl.BlockSpec((rows, d), lambda l, j: (0, 0)),
            pl.BlockSpec((None, d, tn), lambda l, j: (l, 0, j)),
            pl.BlockSpec((None, 1, tn), lambda l, j: (l, 0, j)),
        ],
        out_specs=pl.BlockSpec((None, rows, tn), lambda l, j: (l, 0, j)),
        out_shape=jax.ShapeDtypeStruct((depth, rows, n), F32),
        compiler_params=_cparams("parallel", "parallel"),
        name="modulation",
    )(cond, w_mod, b_mod.reshape(depth, 1, n))
    return out.reshape(depth, rows, N_MOD, d)


def _rope_tables(lay, rot_dim):
    n_rows = lay.seq // GRID_W
    n_freq = rot_dim // 4
    inv_freq = ROPE_THETA ** (-jnp.arange(n_freq, dtype=F32) / n_freq)
    rows = jnp.repeat(jnp.arange(n_rows, dtype=F32), GRID_W)
    cols = jnp.tile(jnp.arange(GRID_W, dtype=F32), n_rows)
    ang = jnp.concatenate([rows[:, None] * inv_freq, cols[:, None] * inv_freq], -1)
    cs, sn = jnp.cos(ang), jnp.sin(ang)
    cos_f = jnp.concatenate([cs, cs], -1)
    sin_f = jnp.concatenate([-sn, sn], -1)
    cos_f = jnp.concatenate([jnp.ones((lay.ctx_len, rot_dim), F32), cos_f], 0)
    sin_f = jnp.concatenate([jnp.zeros((lay.ctx_len, rot_dim), F32), sin_f], 0)
    return cos_f, sin_f


def _even_front_kernel(x_ref, mod_ref, ng_ref, w_ref, qg_ref, kg_ref, cos_ref, sin_ref,
                       q_ref, k_ref, vt_ref, u_ref, ub_ref):
    h = _norm_mod(x_ref[...], ng_ref[...], mod_ref[0:1, :], mod_ref[1:2, :])
    p = jnp.dot(h.astype(BF16), w_ref[...], preferred_element_type=F32)
    cos, sin = cos_ref[...], sin_ref[...]
    hd = A_HEAD_DIM
    a_w, kv_w = A_HEADS * hd, A_KV_HEADS * hd

    def head(t, gain):
        t = _rms(t, gain)
        return t * cos + pltpu.roll(t, hd // 2, 1) * sin

    q_scale = hd ** -0.5 * LOG2E
    for i in range(A_HEADS):
        q_ref[:, i * hd:(i + 1) * hd] = (head(p[:, i * hd:(i + 1) * hd], qg_ref[...]) * q_scale).astype(BF16)
    for i in range(A_KV_HEADS):
        lo = a_w + i * hd
        k_ref[:, i * hd:(i + 1) * hd] = head(p[:, lo:lo + hd], kg_ref[...]).astype(BF16)
        _store_vt(vt_ref, i, p[:, lo + kv_w:lo + kv_w + hd])
    u = p[:, a_w + 2 * kv_w:]
    u_ref[...] = u.astype(BF16)
    for c in range(ROW_TILE // S5_CHUNK):
        ub_ref[c] = u[c * S5_CHUNK:(c + 1) * S5_CHUNK, :].T


def _even_front(lay, xs, mod, norm_g, w_in, qn_g, kn_g, cos_f, sin_f):
    d = xs.shape[1]
    hd = A_HEAD_DIM
    a_w, kv_w = A_HEADS * hd, A_KV_HEADS * hd
    b_w = w_in.shape[1] - a_w - 2 * kv_w
    pb = lambda t: (lay.batch_of(t), lay.pos_tile(t), 0)
    shp = lambda w: jax.ShapeDtypeStruct((lay.batch, lay.per_batch, w), BF16)
    cpt = ROW_TILE // S5_CHUNK
    return pl.pallas_call(
        _even_front_kernel,
        grid=(lay.n_tiles,),
        in_specs=[
            pl.BlockSpec((ROW_TILE, d), lambda t: (t, 0)),
            pl.BlockSpec((None, N_MOD, d), lambda t: (lay.mod_row(t), 0, 0)),
            _const_spec((1, d)),
            _const_spec(w_in.shape),
            _const_spec((1, hd)),
            _const_spec((1, hd)),
            pl.BlockSpec((ROW_TILE, hd), lambda t: (lay.pos_tile(t), 0)),
            pl.BlockSpec((ROW_TILE, hd), lambda t: (lay.pos_tile(t), 0)),
        ],
        out_specs=[
            pl.BlockSpec((None, ROW_TILE, a_w), pb),
            pl.BlockSpec((None, ROW_TILE, kv_w), pb),
            pl.BlockSpec((None, A_KV_HEADS, hd + V_PAD, ROW_TILE),
                         lambda t: (lay.batch_of(t), 0, 0, lay.pos_tile(t))),
            pl.BlockSpec((None, ROW_TILE, b_w), pb),
            pl.BlockSpec((cpt, b_w, S5_CHUNK), lambda t: (lay.chunk_tile(t), 0, 0)),
        ],
        out_shape=[shp(a_w), shp(kv_w),
                   jax.ShapeDtypeStruct((lay.batch, A_KV_HEADS, hd + V_PAD, lay.per_batch), BF16),
                   shp(b_w),
                   jax.ShapeDtypeStruct((lay.batch * lay.per_batch // S5_CHUNK, b_w, S5_CHUNK), F32)],
        compiler_params=_cparams("parallel"),
        name="even_front",
    )(xs, mod, norm_g.reshape(1, d), w_in, qn_g.reshape(1, hd), kn_g.reshape(1, hd), cos_f, sin_f)


def _attn_kernel(q_ref, k_ref, vt_ref, o_ref, s_sc, *, heads, group, dk, dv, ctx_len, kv_chunk, n_chunks):
    is_lat = pl.program_id(2) > 0
    n_cols = heads * group
    lat_chunks = [(0, ctx_len)] + [(ctx_len + c * kv_chunk, kv_chunk) for c in range(n_chunks)]

    def scores(col, start, size):
        hh = col // group
        k = k_ref[start:start + size, hh * dk:(hh + 1) * dk]
        q = q_ref[:, col * dk:(col + 1) * dk]
        s = lax.dot_general(k, q, (((1,), (1,)), ((), ())), preferred_element_type=F32)
        s_sc[col % 2, start:start + size, :] = s
        return jnp.max(s, axis=0, keepdims=True)

    def weighted(col, start, size, m):
        p = jnp.exp2(s_sc[col % 2, start:start + size, :] - m).astype(BF16)
        return jnp.dot(vt_ref[col // group, :, start:start + size], p, preferred_element_type=F32)

    def finish(col, acc):
        out = acc[0:dv, :] / acc[dv:dv + 1, :]
        o_ref[:, col * dv:(col + 1) * dv] = out.T.astype(o_ref.dtype)

    def run(chunks):
        m_cur = None
        for start, size in chunks:
            mx = scores(0, start, size)
            m_cur = mx if m_cur is None else jnp.maximum(m_cur, mx)
        for col in range(n_cols):
            m_next, acc = None, None
            for start, size in chunks:
                if col + 1 < n_cols:
                    mx = scores(col + 1, start, size)
                    m_next = mx if m_next is None else jnp.maximum(m_next, mx)
                pv = weighted(col, start, size, m_cur)
                acc = pv if acc is None else acc + pv
            finish(col, acc)
            m_cur = m_next

    @pl.when(is_lat)
    def _():
        run(lat_chunks)

    @pl.when(jnp.logical_not(is_lat))
    def _():
        run(lat_chunks[:1])


def _attention(lay, q, k, vt, *, kv_heads, heads_per_step, group, dk, dv, kv_chunk=512):
    b, p_len, _ = q.shape
    hb = heads_per_step
    assert lay.ctx_len == ROW_TILE and lay.seq % kv_chunk == 0 and kv_heads % hb == 0
    kern = functools.partial(_attn_kernel, heads=hb, group=group, dk=dk, dv=dv, ctx_len=lay.ctx_len,
                             kv_chunk=kv_chunk, n_chunks=lay.seq // kv_chunk)
    return pl.pallas_call(
        kern,
        grid=(b, kv_heads // hb, p_len // ROW_TILE),
        in_specs=[
            pl.BlockSpec((None, ROW_TILE, hb * group * dk), lambda bi, h, t: (bi, t, h)),
            pl.BlockSpec((None, p_len, hb * dk), lambda bi, h, t: (bi, 0, h)),
            pl.BlockSpec((None, hb, dv + V_PAD, p_len), lambda bi, h, t: (bi, h, 0, 0)),
        ],
        out_specs=pl.BlockSpec((None, ROW_TILE, hb * group * dv), lambda bi, h, t: (bi, t, h)),
        out_shape=jax.ShapeDtypeStruct((b, p_len, kv_heads * group * dv), BF16),
        scratch_shapes=[pltpu.VMEM((2, p_len, ROW_TILE), F32)],
        compiler_params=_cparams("parallel", "parallel", "arbitrary"),
        name="attention",
    )(q, k, vt)


def _s5_operators(lam_re, lam_im, log_dt, b_re, b_im, c_re, c_im, chunk, reverse):
    g, p_dim, cg = b_re.shape
    lam = lax.complex(jnp.minimum(lam_re.astype(F32), -1e-4), lam_im.astype(F32))
    dt = jnp.exp(log_dt.astype(F32))[:, None]
    lam_dt = lam * dt
    lam_bar = jnp.exp(lam_dt)
    b_bar = ((lam_bar - 1.0) / lam)[..., None] * lax.complex(b_re.astype(F32), b_im.astype(F32))
    cmat = lax.complex(c_re.astype(F32), c_im.astype(F32))
    def powers(exponents):
        return jnp.exp(lam_dt[None] * exponents.astype(F32)[:, None, None].astype(jnp.complex64))

    t_idx = jnp.arange(chunk)
    seen_after = t_idx if reverse else chunk - 1 - t_idx
    lags = (chunk - t_idx) % chunk if reverse else t_idx
    kern = jnp.real(jnp.einsum('gcp,tgp,gpi->gict', cmat, powers(lags), b_bar))
    st = jnp.einsum('tgp,gpc->gctp', powers(seen_after), b_bar).reshape(g, cg * chunk, p_dim)
    so = jnp.einsum('gcp,tgp->gpct', cmat, powers(chunk - seen_after)).reshape(g, p_dim, cg * chunk)
    decay = powers(jnp.full((1,), chunk))[0]
    return (kern, jnp.real(st), jnp.imag(st), jnp.real(so), -jnp.imag(so), jnp.real(decay), jnp.imag(decay))


def _group_rows(ub_ref, gi):
    cg = S5_GROUP
    return jnp.concatenate([ub_ref[:, gi * cg + c, :] for c in range(cg)], axis=1).astype(BF16)


def _s5_state_in_kernel(ub_ref, w_ref, o0, o1, o2, o3):
    for gi in range(w_ref.shape[0]):
        x = _group_rows(ub_ref, gi)
        for j, o in enumerate((o0, o1, o2, o3)):
            o[gi] = jnp.dot(x, w_ref[gi, j], preferred_element_type=F32)


def _s5_carry_kernel(sfr, sfi, sbr, sbi, lfr, lfi, lbr, lbi, hfr, hfi, hbr, hbi, *, n_chunks, ctx_chunks, rows):
    a_fr, a_fi, a_br, a_bi = lfr[...], lfi[...], lbr[...], lbi[...]

    def rows_at(kk):
        return pl.ds(kk, rows, stride=n_chunks)

    def body(i, carry):
        f_r, f_i, b_r, b_i = carry
        kf = i
        kb = jnp.where(i < ctx_chunks, ctx_chunks - 1 - i, n_chunks - 1 - (i - ctx_chunks))
        hfr[rows_at(kf), :] = f_r
        hfi[rows_at(kf), :] = f_i
        hbr[rows_at(kb), :] = b_r
        hbi[rows_at(kb), :] = b_i
        nf_r = a_fr * f_r - a_fi * f_i + sfr[rows_at(kf), :]
        nf_i = a_fr * f_i + a_fi * f_r + sfi[rows_at(kf), :]
        nb_r = a_br * b_r - a_bi * b_i + sbr[rows_at(kb), :]
        nb_i = a_br * b_i + a_bi * b_r + sbi[rows_at(kb), :]
        return nf_r, nf_i, nb_r, nb_i

    z = jnp.zeros(a_fr.shape, F32)
    lax.fori_loop(0, n_chunks, body, (z, z, z, z))


def _s5_main_kernel(ub_ref, kf_ref, kb_ref, wo_ref, h0, h1, h2, h3, y_ref, m_ref):
    t, cg = S5_CHUNK, S5_GROUP
    row = lax.broadcasted_iota(jnp.int32, (t, t), 0)
    col = lax.broadcasted_iota(jnp.int32, (t, t), 1)
    causal = col >= row

    def toeplitz(vec):
        return pltpu.roll(jnp.broadcast_to(vec, (t, t)), 0, 1, stride=1, stride_axis=0)

    def build(ci, carry):
        kf, kb = kf_ref[ci], kb_ref[ci]
        r0 = pl.multiple_of(ci * t, t)
        for co in range(cg):
            blk = jnp.where(causal, toeplitz(kf[co:co + 1, :]), toeplitz(kb[co:co + 1, :]))
            m_ref[pl.ds(r0, t), co * t:(co + 1) * t] = blk.astype(BF16)
        return carry

    lax.fori_loop(0, cg, build, 0)
    acc = jnp.dot(_group_rows(ub_ref, 0), m_ref[...], preferred_element_type=F32)
    for j, h in enumerate((h0, h1, h2, h3)):
        acc += jnp.dot(h[...].astype(BF16), wo_ref[j], preferred_element_type=F32)
    for co in range(cg):
        y_ref[:, co, :] = acc[:, co * t:(co + 1) * t]


def _s5_scan(lay, ub, lam_re, lam_im, log_dt, b_re, b_im, c_re, c_im):
    rows, width, t = ub.shape
    cg, p_dim = S5_GROUP, S5_STATE
    g = width // cg
    bsz = lay.batch
    n_chunks = rows // bsz
    ctx_chunks = lay.ctx_len // t
    tc = t * cg
    fwd = _s5_operators(lam_re[0], lam_im[0], log_dt[0], b_re[0], b_im[0], c_re[0], c_im[0], t, False)
    bwd = _s5_operators(lam_re[1], lam_im[1], log_dt[1], b_re[1], b_im[1], c_re[1], c_im[1], t, True)
    kf = fwd[0].at[..., 0].add(bwd[0][..., 0])
    kb = bwd[0]
    w_in = jnp.stack([fwd[1], fwd[2], bwd[1], bwd[2]], axis=1).astype(BF16)
    w_out = jnp.stack([fwd[3], fwd[4], bwd[3], bwd[4]], axis=1).astype(BF16)

    gb = 4
    state_shape = jax.ShapeDtypeStruct((g, rows, p_dim), F32)
    st_spec = pl.BlockSpec((gb, rows, p_dim), lambda i: (i, 0, 0))
    s_in = pl.pallas_call(
        _s5_state_in_kernel,
        grid=(g // gb,),
        in_specs=[pl.BlockSpec((rows, gb * cg, t), lambda i: (0, i, 0)),
                  pl.BlockSpec((gb, 4, tc, p_dim), lambda i: (i, 0, 0, 0))],
        out_specs=[st_spec] * 4,
        out_shape=[state_shape] * 4,
        compiler_params=_cparams("parallel"),
        name="s5_state_in",
    )(ub, w_in)

    gb_c = 8
    crow = gb_c * bsz
    flat = lambda a: a.reshape(g * rows, p_dim)
    dec = [jnp.repeat(a, bsz, axis=0) for a in (fwd[5], fwd[6], bwd[5], bwd[6])]
    carry_spec = pl.BlockSpec((crow * n_chunks, p_dim), lambda i: (i, 0))
    h_in = pl.pallas_call(
        functools.partial(_s5_carry_kernel, n_chunks=n_chunks, ctx_chunks=ctx_chunks, rows=crow),
        grid=(g // gb_c,),
        in_specs=[carry_spec] * 4 + [pl.BlockSpec((crow, p_dim), lambda i: (i, 0))] * 4,
        out_specs=[carry_spec] * 4,
        out_shape=[jax.ShapeDtypeStruct((g * rows, p_dim), F32)] * 4,
        compiler_params=_cparams("parallel"),
        name="s5_carry",
    )(*[flat(a) for a in s_in], *dec)
    h_in = [a.reshape(g, rows, p_dim) for a in h_in]

    h_spec = pl.BlockSpec((None, rows, p_dim), lambda i: (i, 0, 0))
    return pl.pallas_call(
        _s5_main_kernel,
        grid=(g,),
        in_specs=[pl.BlockSpec((rows, cg, t), lambda i: (0, i, 0)),
                  pl.BlockSpec((None, cg, cg, t), lambda i: (i, 0, 0, 0)),
                  pl.BlockSpec((None, cg, cg, t), lambda i: (i, 0, 0, 0)),
                  pl.BlockSpec((None, 4, p_dim, tc), lambda i: (i, 0, 0, 0))] + [h_spec] * 4,
        out_specs=pl.BlockSpec((rows, cg, t), lambda i: (0, i, 0)),
        out_shape=jax.ShapeDtypeStruct((rows, width, t), F32),
        scratch_shapes=[pltpu.VMEM((tc, tc), BF16)],
        compiler_params=_cparams("parallel"),
        name="s5_main",
    )(ub, kf, kb, w_out, *h_in)


def _gelu_tanh(x):
    return 0.5 * x * (1.0 + jnp.tanh(math.sqrt(2.0 / math.pi) * (x + 0.044715 * (x * x * x))))


def _even_back_kernel(a_ref, yb_ref, u_ref, x_ref, mod_ref, d_ref, wg_ref, wo_ref, ng_ref, xo_ref, h_ref):
    y_ssm = jnp.concatenate([yb_ref[c].T for c in range(yb_ref.shape[0])], axis=0)
    y = d_ref[...] * u_ref[...].astype(F32) + y_ssm
    gl = _gelu_tanh(y)
    gate = jnp.dot(gl.astype(BF16), wg_ref[...], preferred_element_type=F32)
    s = gl * jax.nn.sigmoid(gate)
    a_w = a_ref.shape[1]
    o = jnp.dot(a_ref[...], wo_ref[0:a_w, :], preferred_element_type=F32)
    o += jnp.dot(s.astype(BF16), wo_ref[a_w:, :], preferred_element_type=F32)
    x_new = x_ref[...] + mod_ref[2:3, :] * o
    xo_ref[...] = x_new
    h_ref[...] = _norm_mod(x_new, ng_ref[...], mod_ref[3:4, :], mod_ref[4:5, :]).astype(BF16)


def _even_back(lay, attn, yb, u, xs, mod, d_skip, w_glu, w_out, norm_g):
    n, d = xs.shape
    a_w, b_w = attn.shape[2], u.shape[2]
    pb = lambda t: (lay.batch_of(t), lay.pos_tile(t), 0)
    cpt = ROW_TILE // S5_CHUNK
    return pl.pallas_call(
        _even_back_kernel,
        grid=(lay.n_tiles,),
        in_specs=[
            pl.BlockSpec((None, ROW_TILE, a_w), pb),
            pl.BlockSpec((cpt, b_w, S5_CHUNK), lambda t: (lay.chunk_tile(t), 0, 0)),
            pl.BlockSpec((None, ROW_TILE, b_w), pb),
            pl.BlockSpec((ROW_TILE, d), lambda t: (t, 0)),
            pl.BlockSpec((None, N_MOD, d), lambda t: (lay.mod_row(t), 0, 0)),
            _const_spec((1, b_w)),
            _const_spec(w_glu.shape),
            _const_spec(w_out.shape),
            _const_spec((1, d)),
        ],
        out_specs=[pl.BlockSpec((ROW_TILE, d), lambda t: (t, 0)),
                   pl.BlockSpec((ROW_TILE, d), lambda t: (t, 0))],
        out_shape=[jax.ShapeDtypeStruct((n, d), F32), jax.ShapeDtypeStruct((n, d), BF16)],
        compiler_params=_cparams("parallel"),
        name="even_back",
    )(attn, yb, u, xs, mod, d_skip.reshape(1, b_w), w_glu, w_out, norm_g.reshape(1, d))


def _ffn_kernel(h_ref, x_ref, mod_ref, w1_ref, w3_ref, w2_ref, o_ref, acc_ref):
    j = pl.program_id(1)

    @pl.when(j == 0)
    def _():
        acc_ref[...] = jnp.zeros(acc_ref.shape, F32)

    h = h_ref[...]
    a = jnp.dot(h, w1_ref[...], preferred_element_type=F32)
    b = jnp.dot(h, w3_ref[...], preferred_element_type=F32)
    z = (a * jax.nn.sigmoid(a) * b).astype(BF16)
    acc_ref[...] += jnp.dot(z, w2_ref[...], preferred_element_type=F32)

    @pl.when(j == pl.num_programs(1) - 1)
    def _():
        o_ref[...] = x_ref[...] + mod_ref[5:6, :] * acc_ref[...]


def _ffn(lay, h, xs, mod, w1, w3, w2):
    n, d = xs.shape
    f = w1.shape[1]
    tm, tf = FFN_TILE_M, FFN_TILE_F
    assert n % tm == 0 and f % tf == 0 and lay.n_ctx % tm == 0 and lay.seq % tm == 0
    return pl.pallas_call(
        _ffn_kernel,
        grid=(n // tm, f // tf),
        in_specs=[
            pl.BlockSpec((tm, d), lambda i, j: (i, 0)),
            pl.BlockSpec((tm, d), lambda i, j: (i, 0)),
            pl.BlockSpec((None, N_MOD, d), lambda i, j: (lay.mod_row(i, tm), 0, 0)),
            pl.BlockSpec((d, tf), lambda i, j: (0, j)),
            pl.BlockSpec((d, tf), lambda i, j: (0, j)),
            pl.BlockSpec((tf, d), lambda i, j: (j, 0)),
        ],
        out_specs=pl.BlockSpec((tm, d), lambda i, j: (i, 0)),
        out_shape=jax.ShapeDtypeStruct((n, d), F32),
        scratch_shapes=[pltpu.VMEM((tm, d), F32)],
        compiler_params=_cparams("parallel", "arbitrary"),
        name="dense_swiglu",
    )(h, xs, mod, w1, w3, w2)


def _odd_front_a_kernel(x_ref, mod_ref, ng_ref, w_ref, qg_ref, kvg_ref, cq_ref, ckv_ref, kpe_ref):
    h = _norm_mod(x_ref[...], ng_ref[...], mod_ref[0:1, :], mod_ref[1:2, :])
    p = jnp.dot(h.astype(BF16), w_ref[...], preferred_element_type=F32)
    cq_ref[...] = _rms(p[:, :Q_LORA], qg_ref[...]).astype(BF16)
    ckv_ref[...] = _rms(p[:, Q_LORA:Q_LORA + KV_LORA], kvg_ref[...]).astype(BF16)
    kpe_ref[...] = p[:, Q_LORA + KV_LORA:]


def _swap_halves(t, width):
    half = width // 2
    if width == LANE:
        return pltpu.roll(t, half, 1)
    lane = lax.broadcasted_iota(jnp.int32, t.shape, 1)
    return jnp.where(lane % width < half, pltpu.roll(t, LANE - half, 1), pltpu.roll(t, half, 1))


def _segment_rms(t, width, seg_ones):
    ss = jnp.dot(t * t, seg_ones, preferred_element_type=F32, precision=HIGHEST)
    return t * lax.rsqrt(ss * (1.0 / width) + EPS)


def _odd_front_b_kernel(cq_ref, ckv_ref, kpe_ref, wq_ref, wkv_ref, qg_ref, kg_ref, cos_ref, sin_ref, seg_ref,
                        q_ref, k_ref, vt_ref):
    q = jnp.dot(cq_ref[...], wq_ref[...], preferred_element_type=F32)
    kv = jnp.dot(ckv_ref[...], wkv_ref[...], preferred_element_type=F32)
    cos, sin, seg = cos_ref[...], sin_ref[...], seg_ref[...]
    qg_n, qg_r = qg_ref[:, :LANE], qg_ref[:, LANE:]
    kg_n, kg_r = kg_ref[:, :LANE], kg_ref[:, LANE:]
    scale = (C_NOPE + C_ROPE) ** -0.5 * LOG2E
    dkp = 2 * LANE
    lane = lax.broadcasted_iota(jnp.int32, (q.shape[0], LANE), 1)

    def rope(t):
        return t * cos + _swap_halves(t, C_ROPE) * sin

    kpe = rope(_segment_rms(kpe_ref[...], C_ROPE, seg) * kg_r)
    kpe_even = jnp.where(lane < C_ROPE, kpe, 0.0).astype(BF16)
    kpe_odd = jnp.where(lane < C_ROPE, 0.0, pltpu.roll(kpe, C_ROPE, 1)).astype(BF16)
    rope_base = C_HEADS * C_NOPE
    for pair in range(C_HEADS // 2):
        qp = q[:, rope_base + pair * LANE:rope_base + (pair + 1) * LANE]
        qp = (rope(_segment_rms(qp, C_ROPE, seg) * qg_r) * scale).astype(BF16)
        for hh in (2 * pair, 2 * pair + 1):
            qn = _rms(q[:, hh * C_NOPE:(hh + 1) * C_NOPE], qg_n) * scale
            q_ref[:, hh * dkp:hh * dkp + LANE] = qn.astype(BF16)
            q_ref[:, hh * dkp + LANE:(hh + 1) * dkp] = qp
            kn = _rms(kv[:, hh * 2 * LANE:hh * 2 * LANE + C_NOPE], kg_n)
            k_ref[:, hh * dkp:hh * dkp + LANE] = kn.astype(BF16)
            k_ref[:, hh * dkp + LANE:(hh + 1) * dkp] = kpe_even if hh % 2 == 0 else kpe_odd
            _store_vt(vt_ref, hh, kv[:, hh * 2 * LANE + C_NOPE:(hh + 1) * 2 * LANE])


def _odd_front(lay, xs, mod, norm_g, w_in, q_lora_g, kv_lora_g, w_uq, w_ukv, qn_g, kn_g, cos_f, sin_f):
    n, d = xs.shape
    assert C_NOPE == LANE and C_V == LANE and 2 * C_ROPE == LANE
    cq, ckv, kpe = pl.pallas_call(
        _odd_front_a_kernel,
        grid=(lay.n_tiles,),
        in_specs=[
            pl.BlockSpec((ROW_TILE, d), lambda t: (t, 0)),
            pl.BlockSpec((None, N_MOD, d), lambda t: (lay.mod_row(t), 0, 0)),
            _const_spec((1, d)),
            _const_spec(w_in.shape),
            _const_spec((1, Q_LORA)),
            _const_spec((1, KV_LORA)),
        ],
        out_specs=[pl.BlockSpec((ROW_TILE, Q_LORA), lambda t: (t, 0)),
                   pl.BlockSpec((ROW_TILE, KV_LORA), lambda t: (t, 0)),
                   pl.BlockSpec((ROW_TILE, LANE), lambda t: (t, 0))],
        out_shape=[jax.ShapeDtypeStruct((n, Q_LORA), BF16), jax.ShapeDtypeStruct((n, KV_LORA), BF16),
                   jax.ShapeDtypeStruct((n, LANE), F32)],
        compiler_params=_cparams("parallel"),
        name="odd_front_a",
    )(xs, mod, norm_g.reshape(1, d), w_in, q_lora_g.reshape(1, Q_LORA), kv_lora_g.reshape(1, KV_LORA))

    seg = (jnp.arange(LANE)[:, None] // C_ROPE == jnp.arange(LANE)[None, :] // C_ROPE).astype(F32)
    pair_gain = lambda g: jnp.concatenate([g[:C_NOPE], g[C_NOPE:], g[C_NOPE:]]).astype(F32).reshape(1, 2 * LANE)
    pb = lambda t: (lay.batch_of(t), lay.pos_tile(t), 0)
    shp = lambda w: jax.ShapeDtypeStruct((lay.batch, lay.per_batch, w), BF16)
    dkp = 2 * LANE
    return pl.pallas_call(
        _odd_front_b_kernel,
        grid=(lay.n_tiles,),
        in_specs=[
            pl.BlockSpec((ROW_TILE, Q_LORA), lambda t: (t, 0)),
            pl.BlockSpec((ROW_TILE, KV_LORA), lambda t: (t, 0)),
            pl.BlockSpec((ROW_TILE, LANE), lambda t: (t, 0)),
            _const_spec(w_uq.shape),
            _const_spec(w_ukv.shape),
            _const_spec((1, 2 * LANE)),
            _const_spec((1, 2 * LANE)),
            pl.BlockSpec((ROW_TILE, LANE), lambda t: (lay.pos_tile(t), 0)),
            pl.BlockSpec((ROW_TILE, LANE), lambda t: (lay.pos_tile(t), 0)),
            _const_spec((LANE, LANE)),
        ],
        out_specs=[pl.BlockSpec((None, ROW_TILE, C_HEADS * dkp), pb),
                   pl.BlockSpec((None, ROW_TILE, C_HEADS * dkp), pb),
                   pl.BlockSpec((None, C_HEADS, C_V + V_PAD, ROW_TILE),
                                lambda t: (lay.batch_of(t), 0, 0, lay.pos_tile(t)))],
        out_shape=[shp(C_HEADS * dkp), shp(C_HEADS * dkp),
                   jax.ShapeDtypeStruct((lay.batch, C_HEADS, C_V + V_PAD, lay.per_batch), BF16)],
        compiler_params=_cparams("parallel"),
        name="odd_front_b",
    )(cq, ckv, kpe, w_uq, w_ukv, pair_gain(qn_g), pair_gain(kn_g), cos_f, sin_f, seg)


def _odd_back_kernel(a_ref, x_ref, mod_ref, wo_ref, ng_ref, wrh_ref, wrl_ref, xo_ref, h_ref, r_ref):
    o = jnp.dot(a_ref[...], wo_ref[...], preferred_element_type=F32)
    x_new = x_ref[...] + mod_ref[2:3, :] * o
    xo_ref[...] = x_new
    h = _norm_mod(x_new, ng_ref[...], mod_ref[3:4, :], mod_ref[4:5, :])
    h_hi = h.astype(BF16)
    h_ref[...] = h_hi
    h_lo = (h - h_hi.astype(F32)).astype(BF16)
    logits = (jnp.dot(h_hi, wrh_ref[...], preferred_element_type=F32)
              + jnp.dot(h_lo, wrh_ref[...], preferred_element_type=F32)
              + jnp.dot(h_hi, wrl_ref[...], preferred_element_type=F32))
    lane = lax.broadcasted_iota(jnp.int32, logits.shape, 1).astype(F32)
    neg = jnp.float32(-jnp.inf)
    lg = jnp.where(lane < N_EXPERTS, logits, neg)
    m1 = jnp.max(lg, axis=-1, keepdims=True)
    i1 = jnp.min(jnp.where(lg == m1, lane, float(LANE)), axis=-1, keepdims=True)
    lg2 = jnp.where(lane == i1, neg, lg)
    m2 = jnp.max(lg2, axis=-1, keepdims=True)
    i2 = jnp.min(jnp.where(lg2 == m2, lane, float(LANE)), axis=-1, keepdims=True)
    e2 = jnp.exp(m2 - m1)
    g1 = 1.0 / (1.0 + e2)
    g2 = e2 / (1.0 + e2)
    r = jnp.where(lane == 0.0, i1, 0.0)
    r = jnp.where(lane == 1.0, i2, r)
    r = jnp.where(lane == 2.0, g1, r)
    r_ref[...] = jnp.where(lane == 3.0, g2, r)


def _odd_back(lay, attn, xs, mod, w_o, norm_g, w_router, first_tile):
    d = xs.shape[1]
    n = xs.shape[0] - first_tile * ROW_TILE
    a_w = attn.shape[2]
    wr = jnp.zeros((d, LANE), F32).at[:, :N_EXPERTS].set(w_router.astype(F32))
    wr_hi = wr.astype(BF16)
    wr_lo = (wr - wr_hi.astype(F32)).astype(BF16)
    pb = lambda t: (lay.batch_of(t + first_tile), lay.pos_tile(t + first_tile), 0)
    row = lambda w: pl.BlockSpec((ROW_TILE, w), lambda t: (t, 0))
    return pl.pallas_call(
        _odd_back_kernel,
        grid=(n // ROW_TILE,),
        in_specs=[
            pl.BlockSpec((None, ROW_TILE, a_w), pb),
            pl.BlockSpec((ROW_TILE, d), lambda t: (t + first_tile, 0)),
            pl.BlockSpec((None, N_MOD, d), lambda t: (lay.mod_row(t + first_tile), 0, 0)),
            _const_spec(w_o.shape),
            _const_spec((1, d)),
            _const_spec((d, LANE)),
            _const_spec((d, LANE)),
        ],
        out_specs=[row(d), row(d), row(LANE)],
        out_shape=[jax.ShapeDtypeStruct((n, d), F32), jax.ShapeDtypeStruct((n, d), BF16),
                   jax.ShapeDtypeStruct((n, LANE), F32)],
        compiler_params=_cparams("parallel"),
        name="odd_back",
    )(attn, xs, mod, w_o, norm_g.reshape(1, d), wr_hi, wr_lo)


def _moe_kernel(te_ref, tv_ref, h_ref, w1_ref, w3_ref, w2_ref, o_ref, acc_ref):
    r, j = pl.program_id(0), pl.program_id(1)
    last = pl.num_programs(1) - 1
    valid = tv_ref[r] > 0

    @pl.when(j == 0)
    def _():
        acc_ref[...] = jnp.zeros(acc_ref.shape, F32)

    @pl.when(valid)
    def _():
        h = h_ref[...]
        a = jnp.dot(h, w1_ref[...], preferred_element_type=F32)
        b = jnp.dot(h, w3_ref[...], preferred_element_type=F32)
        z = (a * jax.nn.sigmoid(a) * b).astype(BF16)
        acc_ref[...] += jnp.dot(z, w2_ref[...], preferred_element_type=F32)

    @pl.when(j == last)
    def _():
        o_ref[...] = acc_ref[...].astype(o_ref.dtype)


def _moe_ffn(h_sorted, tile_expert, tile_valid, w1, w3, w2, layer):
    s_pad, d = h_sorted.shape
    f = w1.shape[3]
    tm, tf = MOE_TILE, MOE_TILE_F
    n_tiles = s_pad // tm
    jf = lambda r, j, tv: jnp.where(tv[r] > 0, j, f // tf - 1)
    grid_spec = pltpu.PrefetchScalarGridSpec(
        num_scalar_prefetch=2,
        grid=(n_tiles, f // tf),
        in_specs=[
            pl.BlockSpec((tm, d), lambda r, j, te, tv: (r, 0)),
            pl.BlockSpec((None, None, d, tf), lambda r, j, te, tv: (layer, te[r], 0, jf(r, j, tv))),
            pl.BlockSpec((None, None, d, tf), lambda r, j, te, tv: (layer, te[r], 0, jf(r, j, tv))),
            pl.BlockSpec((None, None, tf, d), lambda r, j, te, tv: (layer, te[r], jf(r, j, tv), 0)),
        ],
        out_specs=pl.BlockSpec((tm, d), lambda r, j, te, tv: (r, 0)),
        scratch_shapes=[pltpu.VMEM((tm, d), F32)],
    )
    return pl.pallas_call(
        _moe_kernel,
        grid_spec=grid_spec,
        out_shape=jax.ShapeDtypeStruct((s_pad, d), BF16),
        compiler_params=_cparams("arbitrary", "arbitrary"),
        name="moe_experts",
    )(tile_expert, tile_valid, h_sorted, w1, w3, w2)


def _combine_kernel(x_ref, mod_ref, r_ref, y0_ref, y1_ref, o_ref):
    mix = r_ref[:, 2:3] * y0_ref[...].astype(F32) + r_ref[:, 3:4] * y1_ref[...].astype(F32)
    o_ref[...] = x_ref[...] + mod_ref[5:6, :] * mix


def _moe_combine(lay, xs, mod, route, y0, y1, first_tile):
    n, d = xs.shape
    row = pl.BlockSpec((ROW_TILE, d), lambda t: (t, 0))
    return pl.pallas_call(
        _combine_kernel,
        grid=(n // ROW_TILE,),
        in_specs=[row, pl.BlockSpec((None, N_MOD, d), lambda t: (lay.mod_row(t + first_tile), 0, 0)),
                  pl.BlockSpec((ROW_TILE, LANE), lambda t: (t, 0)), row, row],
        out_specs=row,
        out_shape=jax.ShapeDtypeStruct((n, d), F32),
        compiler_params=_cparams("parallel"),
        name="moe_combine",
    )(xs, mod, route, y0, y1)


def _moe(lay, h, route, xs, mod, w1, w3, w2, layer, first_tile):
    n, d = xs.shape
    tm = MOE_TILE
    n_assign = 2 * n
    n_tiles = n_assign // tm + N_EXPERTS
    s_pad = n_tiles * tm
    expert = route[:, 0:2].astype(jnp.int32).reshape(n_assign)
    onehot = (expert[:, None] == jnp.arange(N_EXPERTS)[None, :]).astype(jnp.int32)
    rank = jnp.take_along_axis(jnp.cumsum(onehot, axis=0) - onehot, expert[:, None], axis=1)[:, 0]
    count = jnp.sum(onehot, axis=0)
    padded = (count + tm - 1) // tm * tm
    end = jnp.cumsum(padded)
    start = end - padded
    slot = start[expert] + rank
    token = jnp.arange(n_assign, dtype=jnp.int32) // 2
    tok_of_slot = jnp.zeros((s_pad,), jnp.int32).at[slot].set(token)
    tile_start = jnp.arange(n_tiles, dtype=jnp.int32) * tm
    tile_expert = jnp.minimum(jnp.sum((tile_start[:, None] >= end[None, :]).astype(jnp.int32), axis=1),
                              N_EXPERTS - 1).astype(jnp.int32)
    tile_valid = (tile_start < end[-1]).astype(jnp.int32)
    h_sorted = h.at[tok_of_slot].get(mode="promise_in_bounds")
    y = _moe_ffn(h_sorted, tile_expert, tile_valid, w1, w3, w2, layer)
    slots = slot.reshape(n, 2)
    y0 = y.at[slots[:, 0]].get(mode="promise_in_bounds")
    y1 = y.at[slots[:, 1]].get(mode="promise_in_bounds")
    return _moe_combine(lay, xs, mod, route, y0, y1, first_tile)


def kernel(x, c, ctx, c_ctx, w_mod, b_mod, norm1_g, norm2_g, ev_w_in, ev_qn_g, ev_kn_g, s5_lam_re, s5_lam_im, s5_log_dt, s5_b_re, s5_b_im, s5_c_re, s5_c_im, s5_d, ev_w_glu, ev_w_out, ffn_w1, ffn_w3, ffn_w2, od_w_in, od_q_lora_g, od_kv_lora_g, od_w_uq, od_w_ukv, od_qn_g, od_kn_g, od_w_o, moe_router, moe_w1, moe_w3, moe_w2):
    bsz, seq, d = x.shape
    ctx_len = ctx.shape[1]
    depth = w_mod.shape[0]
    lay = _Layout(bsz, ctx_len, seq)

    cond = jnp.zeros((8, d), F32).at[:bsz].set(c).at[bsz].set(c_ctx)
    mods = _modulation(cond, w_mod, b_mod)

    cos_a, sin_a = _rope_tables(lay, A_HEAD_DIM)
    cos_c, sin_c = _rope_tables(lay, C_ROPE)
    cos_c = jnp.concatenate([cos_c, cos_c], -1)
    sin_c = jnp.concatenate([sin_c, sin_c], -1)

    nope_cols = (jnp.arange(C_HEADS)[:, None] * (C_NOPE + C_ROPE) + jnp.arange(C_NOPE)[None, :]).reshape(-1)
    rope_cols = (jnp.arange(C_HEADS)[:, None] * (C_NOPE + C_ROPE) + C_NOPE + jnp.arange(C_ROPE)[None, :]).reshape(-1)
    uq_cols = jnp.concatenate([nope_cols, rope_cols])

    moe_w1_b, moe_w3_b, moe_w2_b = moe_w1.astype(BF16), moe_w3.astype(BF16), moe_w2.astype(BF16)

    xs = jnp.concatenate([ctx.reshape(bsz * ctx_len, d), x.reshape(bsz * seq, d)], axis=0)
    for i in range(depth):
        j = i // 2
        mod = mods[i]
        if i % 2 == 0:
            q, k, vt, u, ub = _even_front(lay, xs, mod, norm1_g[i], ev_w_in[j].astype(BF16), ev_qn_g[j],
                                          ev_kn_g[j], cos_a, sin_a)
            attn = _attention(lay, q, k, vt, kv_heads=A_KV_HEADS, heads_per_step=1,
                              group=A_HEADS // A_KV_HEADS, dk=A_HEAD_DIM, dv=A_HEAD_DIM)
            yb = _s5_scan(lay, ub, s5_lam_re[j], s5_lam_im[j], s5_log_dt[j], s5_b_re[j], s5_b_im[j],
                          s5_c_re[j], s5_c_im[j])
            xs, h = _even_back(lay, attn, yb, u, xs, mod, s5_d[j], ev_w_glu[j].astype(BF16),
                               ev_w_out[j].astype(BF16), norm2_g[i])
            xs = _ffn(lay, h, xs, mod, ffn_w1[j].astype(BF16), ffn_w3[j].astype(BF16), ffn_w2[j].astype(BF16))
        else:
            w_in = jnp.pad(od_w_in[j], ((0, 0), (0, LANE - C_ROPE))).astype(BF16)
            q, k, vt = _odd_front(lay, xs, mod, norm1_g[i], w_in, od_q_lora_g[j], od_kv_lora_g[j],
                                  od_w_uq[j][:, uq_cols].astype(BF16), od_w_ukv[j].astype(BF16),
                                  od_qn_g[j], od_kn_g[j], cos_c, sin_c)
            attn = _attention(lay, q, k, vt, kv_heads=C_HEADS, heads_per_step=4, group=1, dk=2 * LANE, dv=C_V)
            first_tile = lay.ctx_tiles if i == depth - 1 else 0
            xs, h, route = _odd_back(lay, attn, xs, mod, od_w_o[j].astype(BF16), norm2_g[i], moe_router[j],
                                     first_tile)
            xs = _moe(lay, h, route, xs, mod, moe_w1_b, moe_w3_b, moe_w2_b, j, first_tile)
    if depth % 2 == 0:
        return xs.reshape(bsz, seq, d)
    return xs[bsz * ctx_len:].reshape(bsz, seq, d)
```

```python
import functools
import math

import jax
import jax.numpy as jnp
from jax import lax
from jax.experimental import pallas as pl
from jax.experimental.pallas import tpu as pltpu

F32 = jnp.float32
BF16 = jnp.bfloat16
HIGHEST = lax.Precision.HIGHEST

EPS = 1e-6
ROPE_THETA = 10000.0
GRID_W = 64
N_MOD = 6
ROW_TILE = 256
LANE = 128
VMEM_LIMIT = 56 * 1024 * 1024
LOG2E = 1.4426950408889634
V_PAD = 16

A_HEADS, A_KV_HEADS, A_HEAD_DIM = 8, 2, 128
S5_GROUP, S5_STATE = 16, 64
S5_CHUNK = LANE
C_HEADS, C_NOPE, C_ROPE, C_V = 16, 128, 64, 128
Q_LORA, KV_LORA = 1536, 512
N_EXPERTS = 8
MOE_TILE, MOE_TILE_F = 512, 1024
FFN_TILE_M, FFN_TILE_F = 512, 512


def _cparams(*sem):
    return pltpu.CompilerParams(dimension_semantics=sem, vmem_limit_bytes=VMEM_LIMIT)


def _const_spec(shape):
    nd = len(shape)
    return pl.BlockSpec(shape, lambda *_: (0,) * nd, pipeline_mode=pl.Buffered(1))


class _Layout:
    def __init__(self, batch, ctx_len, seq):
        assert ctx_len == ROW_TILE and seq % ROW_TILE == 0
        self.batch, self.ctx_len, self.seq = batch, ctx_len, seq
        self.n_ctx = batch * ctx_len
        self.n_tok = batch * (ctx_len + seq)
        self.per_batch = ctx_len + seq
        self.ctx_tiles = self.n_ctx // ROW_TILE
        self.lat_tiles_per_batch = seq // ROW_TILE
        self.n_tiles = self.n_tok // ROW_TILE

    def mod_row(self, t, tile=ROW_TILE):
        ctx_tiles = self.n_ctx // tile
        per_b = self.seq // tile
        return jnp.where(t < ctx_tiles, self.batch, (t - ctx_tiles) // per_b)

    def batch_of(self, t):
        return jnp.where(t < self.ctx_tiles, t, (t - self.ctx_tiles) // self.lat_tiles_per_batch)

    def pos_tile(self, t):
        return jnp.where(t < self.ctx_tiles, 0, 1 + (t - self.ctx_tiles) % self.lat_tiles_per_batch)

    def chunk_tile(self, t):
        return self.batch_of(t) * (self.per_batch // ROW_TILE) + self.pos_tile(t)


def _rms(x, gain):
    return x * lax.rsqrt(jnp.mean(x * x, axis=-1, keepdims=True) + EPS) * gain


def _norm_mod(x, gain, shift, scale):
    return _rms(x, gain) * (1.0 + scale) + shift


def _store_vt(vt_ref, head, v):
    dv = v.shape[1]
    vt_ref[head, 0:dv, :] = v.T.astype(BF16)
    vt_ref[head, dv:dv + V_PAD, :] = jnp.ones((V_PAD, v.shape[0]), BF16)


def _mod_kernel(cond_ref, w_ref, b_ref, o_ref):
    cnd = cond_ref[...]
    act = cnd * jax.nn.sigmoid(cnd)
    o_ref[...] = jnp.dot(act, w_ref[...], preferred_element_type=F32, precision=HIGHEST) + b_ref[...]


def _modulation(cond, w_mod, b_mod):
    depth, d, n = w_mod.shape
    tn = 1024
    rows = cond.shape[0]
    out = pl.pallas_call(
        _mod_kernel,
        grid=(depth, n // tn),
        in_specs=[
            pl.BlockSpec((rows, d), lambda l, j: (0, 0)),
            pl.BlockSpec((None, d, tn), lambda l, j: (l, 0, j)),
            pl.BlockSpec((None, 1, tn), lambda l, j: (l, 0, j)),
        ],
        out_specs=pl.BlockSpec((None, rows, tn), lambda l, j: (l, 0, j)),
        out_shape=jax.ShapeDtypeStruct((depth, rows, n), F32),
        compiler_params=_cparams("parallel", "parallel"),
        name="modulation",
    )(cond, w_mod, b_mod.reshape(depth, 1, n))
    return out.reshape(depth, rows, N_MOD, d)


def _rope_tables(lay, rot_dim):
    n_rows = lay.seq // GRID_W
    n_freq = rot_dim // 4
    inv_freq = ROPE_THETA ** (-jnp.arange(n_freq, dtype=F32) / n_freq)
    rows = jnp.repeat(jnp.arange(n_rows, dtype=F32), GRID_W)
    cols = jnp.tile(jnp.arange(GRID_W, dtype=F32), n_rows)
    ang = jnp.concatenate([rows[:, None] * inv_freq, cols[:, None] * inv_freq], -1)
    cs, sn = jnp.cos(ang), jnp.sin(ang)
    cos_f = jnp.concatenate([cs, cs], -1)
    sin_f = jnp.concatenate([-sn, sn], -1)
    cos_f = jnp.concatenate([jnp.ones((lay.ctx_len, rot_dim), F32), cos_f], 0)
    sin_f = jnp.concatenate([jnp.zeros((lay.ctx_len, rot_dim), F32), sin_f], 0)
    return cos_f, sin_f


def _even_front_kernel(x_ref, mod_ref, ng_ref, w_ref, qg_ref, kg_ref, cos_ref, sin_ref,
                       q_ref, k_ref, vt_ref, u_ref, ub_ref):
    h = _norm_mod(x_ref[...], ng_ref[...], mod_ref[0:1, :], mod_ref[1:2, :])
    p = jnp.dot(h.astype(BF16), w_ref[...], preferred_element_type=F32)
    cos, sin = cos_ref[...], sin_ref[...]
    hd = A_HEAD_DIM
    a_w, kv_w = A_HEADS * hd, A_KV_HEADS * hd

    def head(t, gain):
        t = _rms(t, gain)
        return t * cos + pltpu.roll(t, hd // 2, 1) * sin

    q_scale = hd ** -0.5 * LOG2E
    for i in range(A_HEADS):
        q_ref[:, i * hd:(i + 1) * hd] = (head(p[:, i * hd:(i + 1) * hd], qg_ref[...]) * q_scale).astype(BF16)
    for i in range(A_KV_HEADS):
        lo = a_w + i * hd
        k_ref[:, i * hd:(i + 1) * hd] = head(p[:, lo:lo + hd], kg_ref[...]).astype(BF16)
        _store_vt(vt_ref, i, p[:, lo + kv_w:lo + kv_w + hd])
    u = p[:, a_w + 2 * kv_w:]
    u_ref[...] = u.astype(BF16)
    for c in range(ROW_TILE // S5_CHUNK):
        ub_ref[c] = u[c * S5_CHUNK:(c + 1) * S5_CHUNK, :].T


def _even_front(lay, xs, mod, norm_g, w_in, qn_g, kn_g, cos_f, sin_f):
    d = xs.shape[1]
    hd = A_HEAD_DIM
    a_w, kv_w = A_HEADS * hd, A_KV_HEADS * hd
    b_w = w_in.shape[1] - a_w - 2 * kv_w
    pb = lambda t: (lay.batch_of(t), lay.pos_tile(t), 0)
    shp = lambda w: jax.ShapeDtypeStruct((lay.batch, lay.per_batch, w), BF16)
    cpt = ROW_TILE // S5_CHUNK
    return pl.pallas_call(
        _even_front_kernel,
        grid=(lay.n_tiles,),
        in_specs=[
            pl.BlockSpec((ROW_TILE, d), lambda t: (t, 0)),
            pl.BlockSpec((None, N_MOD, d), lambda t: (lay.mod_row(t), 0, 0)),
            _const_spec((1, d)),
            _const_spec(w_in.shape),
            _const_spec((1, hd)),
            _const_spec((1, hd)),
            pl.BlockSpec((ROW_TILE, hd), lambda t: (lay.pos_tile(t), 0)),
            pl.BlockSpec((ROW_TILE, hd), lambda t: (lay.pos_tile(t), 0)),
        ],
        out_specs=[
            pl.BlockSpec((None, ROW_TILE, a_w), pb),
            pl.BlockSpec((None, ROW_TILE, kv_w), pb),
            pl.BlockSpec((None, A_KV_HEADS, hd + V_PAD, ROW_TILE),
                         lambda t: (lay.batch_of(t), 0, 0, lay.pos_tile(t))),
            pl.BlockSpec((None, ROW_TILE, b_w), pb),
            pl.BlockSpec((cpt, b_w, S5_CHUNK), lambda t: (lay.chunk_tile(t), 0, 0)),
        ],
        out_shape=[shp(a_w), shp(kv_w),
                   jax.ShapeDtypeStruct((lay.batch, A_KV_HEADS, hd + V_PAD, lay.per_batch), BF16),
                   shp(b_w),
                   jax.ShapeDtypeStruct((lay.batch * lay.per_batch // S5_CHUNK, b_w, S5_CHUNK), F32)],
        compiler_params=_cparams("parallel"),
        name="even_front",
    )(xs, mod, norm_g.reshape(1, d), w_in, qn_g.reshape(1, hd), kn_g.reshape(1, hd), cos_f, sin_f)


def _attn_kernel(q_ref, k_ref, vt_ref, o_ref, s_sc, *, heads, group, dk, dv, ctx_len, kv_chunk, n_chunks):
    is_lat = pl.program_id(2) > 0
    n_cols = heads * group
    lat_chunks = [(0, ctx_len)] + [(ctx_len + c * kv_chunk, kv_chunk) for c in range(n_chunks)]

    def scores(col, start, size):
        hh = col // group
        k = k_ref[start:start + size, hh * dk:(hh + 1) * dk]
        q = q_ref[:, col * dk:(col + 1) * dk]
        s = lax.dot_general(k, q, (((1,), (1,)), ((), ())), preferred_element_type=F32)
        s_sc[col % 2, start:start + size, :] = s
        return jnp.max(s, axis=0, keepdims=True)

    def weighted(col, start, size, m):
        p = jnp.exp2(s_sc[col % 2, start:start + size, :] - m).astype(BF16)
        return jnp.dot(vt_ref[col // group, :, start:start + size], p, preferred_element_type=F32)

    def finish(col, acc):
        out = acc[0:dv, :] / acc[dv:dv + 1, :]
        o_ref[:, col * dv:(col + 1) * dv] = out.T.astype(o_ref.dtype)

    def run(chunks):
        m_cur = None
        for start, size in chunks:
            mx = scores(0, start, size)
            m_cur = mx if m_cur is None else jnp.maximum(m_cur, mx)
        for col in range(n_cols):
            m_next, acc = None, None
            for start, size in chunks:
                if col + 1 < n_cols:
                    mx = scores(col + 1, start, size)
                    m_next = mx if m_next is None else jnp.maximum(m_next, mx)
                pv = weighted(col, start, size, m_cur)
                acc = pv if acc is None else acc + pv
            finish(col, acc)
            m_cur = m_next

    @pl.when(is_lat)
    def _():
        run(lat_chunks)

    @pl.when(jnp.logical_not(is_lat))
    def _():
        run(lat_chunks[:1])


def _attention(lay, q, k, vt, *, kv_heads, heads_per_step, group, dk, dv, kv_chunk=1024):
    b, p_len, _ = q.shape
    hb = heads_per_step
    kv_chunk = math.gcd(lay.seq, kv_chunk)
    assert lay.ctx_len == ROW_TILE and kv_heads % hb == 0
    kern = functools.partial(_attn_kernel, heads=hb, group=group, dk=dk, dv=dv, ctx_len=lay.ctx_len,
                             kv_chunk=kv_chunk, n_chunks=lay.seq // kv_chunk)
    return pl.pallas_call(
        kern,
        grid=(b, kv_heads // hb, p_len // ROW_TILE),
        in_specs=[
            pl.BlockSpec((None, ROW_TILE, hb * group * dk), lambda bi, h, t: (bi, t, h)),
            pl.BlockSpec((None, p_len, hb * dk), lambda bi, h, t: (bi, 0, h)),
            pl.BlockSpec((None, hb, dv + V_PAD, p_len), lambda bi, h, t: (bi, h, 0, 0)),
        ],
        out_specs=pl.BlockSpec((None, ROW_TILE, hb * group * dv), lambda bi, h, t: (bi, t, h)),
        out_shape=jax.ShapeDtypeStruct((b, p_len, kv_heads * group * dv), BF16),
        scratch_shapes=[pltpu.VMEM((2, p_len, ROW_TILE), F32)],
        compiler_params=_cparams("parallel", "parallel", "arbitrary"),
        name="attention",
    )(q, k, vt)


def _s5_operators(lam_re, lam_im, log_dt, b_re, b_im, c_re, c_im, chunk, reverse):
    g, p_dim, cg = b_re.shape
    lr, li = jnp.minimum(lam_re.astype(F32), -1e-4), lam_im.astype(F32)
    dt = jnp.exp(log_dt.astype(F32))[:, None]
    growth, omega = lr * dt, li * dt

    def powers(exponents):
        e = exponents.astype(F32)[:, None, None]
        mag, ang = jnp.exp(growth[None] * e), omega[None] * e
        return mag * jnp.cos(ang), mag * jnp.sin(ang)

    one_re, one_im = powers(jnp.ones((1,)))
    x, y = one_re[0] - 1.0, one_im[0]
    den = lr * lr + li * li
    coef_re, coef_im = (x * lr + y * li) / den, (y * lr - x * li) / den
    br, bi = b_re.astype(F32), b_im.astype(F32)
    bb_re = coef_re[..., None] * br - coef_im[..., None] * bi
    bb_im = coef_re[..., None] * bi + coef_im[..., None] * br
    cr, ci = c_re.astype(F32), c_im.astype(F32)

    t_idx = jnp.arange(chunk)
    seen_after = t_idx if reverse else chunk - 1 - t_idx
    lags = (chunk - t_idx) % chunk if reverse else t_idx
    kr, ki = powers(lags)
    d_re = kr[..., None] * bb_re[None] - ki[..., None] * bb_im[None]
    d_im = kr[..., None] * bb_im[None] + ki[..., None] * bb_re[None]
    kern = (jnp.einsum('gcp,tgpi->gict', cr, d_re, precision=HIGHEST)
            - jnp.einsum('gcp,tgpi->gict', ci, d_im, precision=HIGHEST))
    sr, si = powers(seen_after)
    sr, si = sr.transpose(1, 0, 2)[:, None], si.transpose(1, 0, 2)[:, None]
    tr, ti = bb_re.transpose(0, 2, 1)[:, :, None], bb_im.transpose(0, 2, 1)[:, :, None]
    st = jnp.concatenate([sr * tr - si * ti, sr * ti + si * tr], axis=-1)
    qr, qi = powers(chunk - seen_after)
    qr, qi = qr.transpose(1, 2, 0)[:, :, None], qi.transpose(1, 2, 0)[:, :, None]
    ur, ui = cr.transpose(0, 2, 1)[..., None], ci.transpose(0, 2, 1)[..., None]
    so = jnp.concatenate([ur * qr - ui * qi, -(ur * qi + ui * qr)], axis=1)
    dr, di = powers(jnp.full((1,), chunk))
    return (kern, st.reshape(g, cg * chunk, 2 * p_dim), so.reshape(g, 2 * p_dim, cg * chunk), dr[0], di[0])


def _group_rows(ub_ref, gi):
    cg = S5_GROUP
    return jnp.concatenate([ub_ref[:, gi * cg + c, :] for c in range(cg)], axis=1).astype(BF16)


def _s5_state_in_kernel(ub_ref, w_ref, o0, o1, o2, o3):
    p_dim = o0.shape[2]
    for gi in range(w_ref.shape[0]):
        s = jnp.dot(_group_rows(ub_ref, gi), w_ref[gi], preferred_element_type=F32)
        for j, o in enumerate((o0, o1, o2, o3)):
            o[gi] = s[:, j * p_dim:(j + 1) * p_dim]


def _s5_carry_kernel(sfr, sfi, sbr, sbi, lfr, lfi, lbr, lbi, hfr, hfi, hbr, hbi, *, n_chunks, ctx_chunks, rows):
    a_fr, a_fi, a_br, a_bi = lfr[...], lfi[...], lbr[...], lbi[...]

    def rows_at(kk):
        return pl.ds(kk, rows, stride=n_chunks)

    def body(i, carry):
        f_r, f_i, b_r, b_i = carry
        kf = i
        kb = jnp.where(i < ctx_chunks, ctx_chunks - 1 - i, n_chunks - 1 - (i - ctx_chunks))
        hfr[rows_at(kf), :] = f_r
        hfi[rows_at(kf), :] = f_i
        hbr[rows_at(kb), :] = b_r
        hbi[rows_at(kb), :] = b_i
        nf_r = a_fr * f_r - a_fi * f_i + sfr[rows_at(kf), :]
        nf_i = a_fr * f_i + a_fi * f_r + sfi[rows_at(kf), :]
        nb_r = a_br * b_r - a_bi * b_i + sbr[rows_at(kb), :]
        nb_i = a_br * b_i + a_bi * b_r + sbi[rows_at(kb), :]
        return nf_r, nf_i, nb_r, nb_i

    z = jnp.zeros(a_fr.shape, F32)
    lax.fori_loop(0, n_chunks, body, (z, z, z, z))


def _s5_main_kernel(ub_ref, kf_ref, kb_ref, wo_ref, h0, h1, h2, h3, y_ref, m_ref):
    t, cg = S5_CHUNK, S5_GROUP
    row = lax.broadcasted_iota(jnp.int32, (t, t), 0)
    col = lax.broadcasted_iota(jnp.int32, (t, t), 1)
    causal = col >= row

    def toeplitz(vec):
        return pltpu.roll(jnp.broadcast_to(vec, (t, t)), 0, 1, stride=1, stride_axis=0)

    def build(ci, carry):
        kf, kb = kf_ref[ci], kb_ref[ci]
        r0 = pl.multiple_of(ci * t, t)
        for co in range(cg):
            blk = jnp.where(causal, toeplitz(kf[co:co + 1, :]), toeplitz(kb[co:co + 1, :]))
            m_ref[pl.ds(r0, t), co * t:(co + 1) * t] = blk.astype(BF16)
        return carry

    lax.fori_loop(0, cg, build, 0)
    acc = jnp.dot(_group_rows(ub_ref, 0), m_ref[...], preferred_element_type=F32)
    p_dim = h0.shape[1]
    for j, h in enumerate((h0, h1, h2, h3)):
        acc += jnp.dot(h[...].astype(BF16), wo_ref[j * p_dim:(j + 1) * p_dim, :], preferred_element_type=F32)
    for co in range(cg):
        y_ref[:, co, :] = acc[:, co * t:(co + 1) * t]


def _s5_scan(lay, ub, lam_re, lam_im, log_dt, b_re, b_im, c_re, c_im):
    rows, width, t = ub.shape
    cg, p_dim = S5_GROUP, S5_STATE
    g = width // cg
    bsz = lay.batch
    n_chunks = rows // bsz
    ctx_chunks = lay.ctx_len // t
    tc = t * cg
    fwd = _s5_operators(lam_re[0], lam_im[0], log_dt[0], b_re[0], b_im[0], c_re[0], c_im[0], t, False)
    bwd = _s5_operators(lam_re[1], lam_im[1], log_dt[1], b_re[1], b_im[1], c_re[1], c_im[1], t, True)
    kf = fwd[0].at[..., 0].add(bwd[0][..., 0])
    kb = bwd[0]
    w_in = jnp.concatenate([fwd[1], bwd[1]], axis=2).astype(BF16)
    w_out = jnp.concatenate([fwd[2], bwd[2]], axis=1).astype(BF16)

    gb = 4
    state_shape = jax.ShapeDtypeStruct((g, rows, p_dim), F32)
    st_spec = pl.BlockSpec((gb, rows, p_dim), lambda i: (i, 0, 0))
    s_in = pl.pallas_call(
        _s5_state_in_kernel,
        grid=(g // gb,),
        in_specs=[pl.BlockSpec((rows, gb * cg, t), lambda i: (0, i, 0)),
                  pl.BlockSpec((gb, tc, 4 * p_dim), lambda i: (i, 0, 0))],
        out_specs=[st_spec] * 4,
        out_shape=[state_shape] * 4,
        compiler_params=_cparams("parallel"),
        name="s5_state_in",
    )(ub, w_in)

    gb_c = 8
    crow = gb_c * bsz
    flat = lambda a: a.reshape(g * rows, p_dim)
    dec = [jnp.repeat(a, bsz, axis=0) for a in (fwd[3], fwd[4], bwd[3], bwd[4])]
    carry_spec = pl.BlockSpec((crow * n_chunks, p_dim), lambda i: (i, 0))
    h_in = pl.pallas_call(
        functools.partial(_s5_carry_kernel, n_chunks=n_chunks, ctx_chunks=ctx_chunks, rows=crow),
        grid=(g // gb_c,),
        in_specs=[carry_spec] * 4 + [pl.BlockSpec((crow, p_dim), lambda i: (i, 0))] * 4,
        out_specs=[carry_spec] * 4,
        out_shape=[jax.ShapeDtypeStruct((g * rows, p_dim), F32)] * 4,
        compiler_params=_cparams("parallel"),
        name="s5_carry",
    )(*[flat(a) for a in s_in], *dec)
    h_in = [a.reshape(g, rows, p_dim) for a in h_in]

    h_spec = pl.BlockSpec((None, rows, p_dim), lambda i: (i, 0, 0))
    return pl.pallas_call(
        _s5_main_kernel,
        grid=(g,),
        in_specs=[pl.BlockSpec((rows, cg, t), lambda i: (0, i, 0)),
                  pl.BlockSpec((None, cg, cg, t), lambda i: (i, 0, 0, 0)),
                  pl.BlockSpec((None, cg, cg, t), lambda i: (i, 0, 0, 0)),
                  pl.BlockSpec((None, 4 * p_dim, tc), lambda i: (i, 0, 0))] + [h_spec] * 4,
        out_specs=pl.BlockSpec((rows, cg, t), lambda i: (0, i, 0)),
        out_shape=jax.ShapeDtypeStruct((rows, width, t), F32),
        scratch_shapes=[pltpu.VMEM((tc, tc), BF16)],
        compiler_params=_cparams("parallel"),
        name="s5_main",
    )(ub, kf, kb, w_out, *h_in)


def _gelu_tanh(x):
    return 0.5 * x * (1.0 + jnp.tanh(math.sqrt(2.0 / math.pi) * (x + 0.044715 * (x * x * x))))


def _even_back_kernel(a_ref, yb_ref, u_ref, x_ref, mod_ref, d_ref, wg_ref, wo_ref, ng_ref, xo_ref, h_ref):
    y_ssm = jnp.concatenate([yb_ref[c].T for c in range(yb_ref.shape[0])], axis=0)
    y = d_ref[...] * u_ref[...].astype(F32) + y_ssm
    gl = _gelu_tanh(y)
    gate = jnp.dot(gl.astype(BF16), wg_ref[...], preferred_element_type=F32)
    s = gl * jax.nn.sigmoid(gate)
    a_w = a_ref.shape[1]
    o = jnp.dot(a_ref[...], wo_ref[0:a_w, :], preferred_element_type=F32)
    o += jnp.dot(s.astype(BF16), wo_ref[a_w:, :], preferred_element_type=F32)
    x_new = x_ref[...] + mod_ref[2:3, :] * o
    xo_ref[...] = x_new
    h_ref[...] = _norm_mod(x_new, ng_ref[...], mod_ref[3:4, :], mod_ref[4:5, :]).astype(BF16)


def _even_back(lay, attn, yb, u, xs, mod, d_skip, w_glu, w_out, norm_g):
    n, d = xs.shape
    a_w, b_w = attn.shape[2], u.shape[2]
    pb = lambda t: (lay.batch_of(t), lay.pos_tile(t), 0)
    cpt = ROW_TILE // S5_CHUNK
    return pl.pallas_call(
        _even_back_kernel,
        grid=(lay.n_tiles,),
        in_specs=[
            pl.BlockSpec((None, ROW_TILE, a_w), pb),
            pl.BlockSpec((cpt, b_w, S5_CHUNK), lambda t: (lay.chunk_tile(t), 0, 0)),
            pl.BlockSpec((None, ROW_TILE, b_w), pb),
            pl.BlockSpec((ROW_TILE, d), lambda t: (t, 0)),
            pl.BlockSpec((None, N_MOD, d), lambda t: (lay.mod_row(t), 0, 0)),
            _const_spec((1, b_w)),
            _const_spec(w_glu.shape),
            _const_spec(w_out.shape),
            _const_spec((1, d)),
        ],
        out_specs=[pl.BlockSpec((ROW_TILE, d), lambda t: (t, 0)),
                   pl.BlockSpec((ROW_TILE, d), lambda t: (t, 0))],
        out_shape=[jax.ShapeDtypeStruct((n, d), F32), jax.ShapeDtypeStruct((n, d), BF16)],
        compiler_params=_cparams("parallel"),
        name="even_back",
    )(attn, yb, u, xs, mod, d_skip.reshape(1, b_w), w_glu, w_out, norm_g.reshape(1, d))


def _swiglu_partial(h, w1_ref, w3_ref, w2_ref):
    tf = w1_ref.shape[1]
    step = 2 * LANE
    out = None
    for lo in range(0, tf, step):
        a = jnp.dot(h, w1_ref[:, lo:lo + step], preferred_element_type=F32)
        b = jnp.dot(h, w3_ref[:, lo:lo + step], preferred_element_type=F32)
        z = (a * jax.nn.sigmoid(a) * b).astype(BF16)
        part = jnp.dot(z, w2_ref[lo:lo + step, :], preferred_element_type=F32)
        out = part if out is None else out + part
    return out


def _ffn_kernel(h_ref, x_ref, mod_ref, w1_ref, w3_ref, w2_ref, o_ref, acc_ref):
    j = pl.program_id(1)

    @pl.when(j == 0)
    def _():
        acc_ref[...] = jnp.zeros(acc_ref.shape, F32)

    h = h_ref[...]
    acc_ref[...] += _swiglu_partial(h, w1_ref, w3_ref, w2_ref)

    @pl.when(j == pl.num_programs(1) - 1)
    def _():
        o_ref[...] = x_ref[...] + mod_ref[5:6, :] * acc_ref[...]


def _ffn(lay, h, xs, mod, w1, w3, w2):
    n, d = xs.shape
    f = w1.shape[1]
    tm, tf = FFN_TILE_M, FFN_TILE_F
    assert n % tm == 0 and f % tf == 0 and lay.n_ctx % tm == 0 and lay.seq % tm == 0
    return pl.pallas_call(
        _ffn_kernel,
        grid=(n // tm, f // tf),
        in_specs=[
            pl.BlockSpec((tm, d), lambda i, j: (i, 0)),
            pl.BlockSpec((tm, d), lambda i, j: (i, 0)),
            pl.BlockSpec((None, N_MOD, d), lambda i, j: (lay.mod_row(i, tm), 0, 0)),
            pl.BlockSpec((d, tf), lambda i, j: (0, j)),
            pl.BlockSpec((d, tf), lambda i, j: (0, j)),
            pl.BlockSpec((tf, d), lambda i, j: (j, 0)),
        ],
        out_specs=pl.BlockSpec((tm, d), lambda i, j: (i, 0)),
        out_shape=jax.ShapeDtypeStruct((n, d), F32),
        scratch_shapes=[pltpu.VMEM((tm, d), F32)],
        compiler_params=_cparams("parallel", "arbitrary"),
        name="dense_swiglu",
    )(h, xs, mod, w1, w3, w2)


def _swap_halves(t, width):
    half = width // 2
    if width == LANE:
        return pltpu.roll(t, half, 1)
    lane = lax.broadcasted_iota(jnp.int32, t.shape, 1)
    return jnp.where(lane % width < half, pltpu.roll(t, LANE - half, 1), pltpu.roll(t, half, 1))


def _segment_rms(t, width, seg_ones):
    ss = jnp.dot(t * t, seg_ones, preferred_element_type=F32, precision=HIGHEST)
    return t * lax.rsqrt(ss * (1.0 / width) + EPS)


def _odd_front_kernel(x_ref, mod_ref, ng_ref, win_ref, qlg_ref, kvlg_ref, wq_ref, wkv_ref, qg_ref, kg_ref,
                      cos_ref, sin_ref, seg_ref, q_ref, k_ref, vt_ref):
    h = _norm_mod(x_ref[...], ng_ref[...], mod_ref[0:1, :], mod_ref[1:2, :])
    p = jnp.dot(h.astype(BF16), win_ref[...], preferred_element_type=F32)
    cq = _rms(p[:, :Q_LORA], qlg_ref[...]).astype(BF16)
    ckv = _rms(p[:, Q_LORA:Q_LORA + KV_LORA], kvlg_ref[...]).astype(BF16)
    kpe_raw = p[:, Q_LORA + KV_LORA:]
    q = jnp.dot(cq, wq_ref[...], preferred_element_type=F32)
    kv = jnp.dot(ckv, wkv_ref[...], preferred_element_type=F32)
    cos, sin, seg = cos_ref[...], sin_ref[...], seg_ref[...]
    qg_n, qg_r = qg_ref[:, :LANE], qg_ref[:, LANE:]
    kg_n, kg_r = kg_ref[:, :LANE], kg_ref[:, LANE:]
    scale = (C_NOPE + C_ROPE) ** -0.5 * LOG2E
    dkp = 2 * LANE
    lane = lax.broadcasted_iota(jnp.int32, (q.shape[0], LANE), 1)

    def rope(t):
        return t * cos + _swap_halves(t, C_ROPE) * sin

    kpe = rope(_segment_rms(kpe_raw, C_ROPE, seg) * kg_r)
    kpe_even = jnp.where(lane < C_ROPE, kpe, 0.0).astype(BF16)
    kpe_odd = jnp.where(lane < C_ROPE, 0.0, pltpu.roll(kpe, C_ROPE, 1)).astype(BF16)
    rope_base = C_HEADS * C_NOPE
    for pair in range(C_HEADS // 2):
        qp = q[:, rope_base + pair * LANE:rope_base + (pair + 1) * LANE]
        qp = (rope(_segment_rms(qp, C_ROPE, seg) * qg_r) * scale).astype(BF16)
        for hh in (2 * pair, 2 * pair + 1):
            qn = _rms(q[:, hh * C_NOPE:(hh + 1) * C_NOPE], qg_n) * scale
            q_ref[:, hh * dkp:hh * dkp + LANE] = qn.astype(BF16)
            q_ref[:, hh * dkp + LANE:(hh + 1) * dkp] = qp
            kn = _rms(kv[:, hh * 2 * LANE:hh * 2 * LANE + C_NOPE], kg_n)
            k_ref[:, hh * dkp:hh * dkp + LANE] = kn.astype(BF16)
            k_ref[:, hh * dkp + LANE:(hh + 1) * dkp] = kpe_even if hh % 2 == 0 else kpe_odd
            _store_vt(vt_ref, hh, kv[:, hh * 2 * LANE + C_NOPE:(hh + 1) * 2 * LANE])


def _odd_front(lay, xs, mod, norm_g, w_in, q_lora_g, kv_lora_g, w_uq, w_ukv, qn_g, kn_g, cos_f, sin_f):
    n, d = xs.shape
    assert C_NOPE == LANE and C_V == LANE and 2 * C_ROPE == LANE
    seg =(jnp.arange(LANE)[:, None] // C_ROPE == jnp.arange(LANE)[None, :] // C_ROPE).astype(F32)
    pair_gain = lambda g: jnp.concatenate([g[:C_NOPE], g[C_NOPE:], g[C_NOPE:]]).astype(F32).reshape(1, 2 * LANE)
    pb = lambda t: (lay.batch_of(t), lay.pos_tile(t), 0)
    shp = lambda w: jax.ShapeDtypeStruct((lay.batch, lay.per_batch, w), BF16)
    dkp = 2 * LANE
    return pl.pallas_call(
        _odd_front_kernel,
        grid=(lay.n_tiles,),
        in_specs=[
            pl.BlockSpec((ROW_TILE, d), lambda t: (t, 0)),
            pl.BlockSpec((None, N_MOD, d), lambda t: (lay.mod_row(t), 0, 0)),
            _const_spec((1, d)),
            _const_spec(w_in.shape),
            _const_spec((1, Q_LORA)),
            _const_spec((1, KV_LORA)),
            _const_spec(w_uq.shape),
            _const_spec(w_ukv.shape),
            _const_spec((1, 2 * LANE)),
            _const_spec((1, 2 * LANE)),
            pl.BlockSpec((ROW_TILE, LANE), lambda t: (lay.pos_tile(t), 0)),
            pl.BlockSpec((ROW_TILE, LANE), lambda t: (lay.pos_tile(t), 0)),
            _const_spec((LANE, LANE)),
        ],
        out_specs=[pl.BlockSpec((None, ROW_TILE, C_HEADS * dkp), pb),
                   pl.BlockSpec((None, ROW_TILE, C_HEADS * dkp), pb),
                   pl.BlockSpec((None, C_HEADS, C_V + V_PAD, ROW_TILE),
                                lambda t: (lay.batch_of(t), 0, 0, lay.pos_tile(t)))],
        out_shape=[shp(C_HEADS * dkp), shp(C_HEADS * dkp),
                   jax.ShapeDtypeStruct((lay.batch, C_HEADS, C_V + V_PAD, lay.per_batch), BF16)],
        compiler_params=_cparams("parallel"),
        name="odd_front",
    )(xs, mod, norm_g.reshape(1, d), w_in, q_lora_g.reshape(1, Q_LORA), kv_lora_g.reshape(1, KV_LORA),
      w_uq, w_ukv, pair_gain(qn_g), pair_gain(kn_g), cos_f, sin_f, seg)


def _odd_back_kernel(a_ref, x_ref, mod_ref, wo_ref, ng_ref, wrh_ref, wrl_ref, xo_ref, h_ref, r_ref):
    o = jnp.dot(a_ref[...], wo_ref[...], preferred_element_type=F32)
    x_new = x_ref[...] + mod_ref[2:3, :] * o
    xo_ref[...] = x_new
    h = _norm_mod(x_new, ng_ref[...], mod_ref[3:4, :], mod_ref[4:5, :])
    h_hi = h.astype(BF16)
    h_ref[...] = h_hi
    h_lo = (h - h_hi.astype(F32)).astype(BF16)
    logits = (jnp.dot(h_hi, wrh_ref[...], preferred_element_type=F32)
              + jnp.dot(h_lo, wrh_ref[...], preferred_element_type=F32)
              + jnp.dot(h_hi, wrl_ref[...], preferred_element_type=F32))
    lane = lax.broadcasted_iota(jnp.int32, logits.shape, 1).astype(F32)
    neg = jnp.float32(-jnp.inf)
    lg = jnp.where(lane < N_EXPERTS, logits, neg)
    m1 = jnp.max(lg, axis=-1, keepdims=True)
    i1 = jnp.min(jnp.where(lg == m1, lane, float(LANE)), axis=-1, keepdims=True)
    lg2 = jnp.where(lane == i1, neg, lg)
    m2 = jnp.max(lg2, axis=-1, keepdims=True)
    i2 = jnp.min(jnp.where(lg2 == m2, lane, float(LANE)), axis=-1, keepdims=True)
    e2 = jnp.exp(m2 - m1)
    g1 = 1.0 / (1.0 + e2)
    g2 = e2 / (1.0 + e2)
    r = jnp.where(lane == 0.0, i1, 0.0)
    r = jnp.where(lane == 1.0, i2, r)
    r = jnp.where(lane == 2.0, g1, r)
    r_ref[...] = jnp.where(lane == 3.0, g2, r)


def _odd_back(lay, attn, xs, mod, w_o, norm_g, w_router, first_tile):
    d = xs.shape[1]
    n = xs.shape[0] - first_tile * ROW_TILE
    a_w = attn.shape[2]
    wr = jnp.zeros((d, LANE), F32).at[:, :N_EXPERTS].set(w_router.astype(F32))
    wr_hi = wr.astype(BF16)
    wr_lo = (wr - wr_hi.astype(F32)).astype(BF16)
    pb = lambda t: (lay.batch_of(t + first_tile), lay.pos_tile(t + first_tile), 0)
    row = lambda w: pl.BlockSpec((ROW_TILE, w), lambda t: (t, 0))
    return pl.pallas_call(
        _odd_back_kernel,
        grid=(n // ROW_TILE,),
        in_specs=[
            pl.BlockSpec((None, ROW_TILE, a_w), pb),
            pl.BlockSpec((ROW_TILE, d), lambda t: (t + first_tile, 0)),
            pl.BlockSpec((None, N_MOD, d), lambda t: (lay.mod_row(t + first_tile), 0, 0)),
            _const_spec(w_o.shape),
            _const_spec((1, d)),
            _const_spec((d, LANE)),
            _const_spec((d, LANE)),
        ],
        out_specs=[row(d), row(d), row(LANE)],
        out_shape=[jax.ShapeDtypeStruct((n, d), F32), jax.ShapeDtypeStruct((n, d), BF16),
                   jax.ShapeDtypeStruct((n, LANE), F32)],
        compiler_params=_cparams("parallel"),
        name="odd_back",
    )(attn, xs, mod, w_o, norm_g.reshape(1, d), wr_hi, wr_lo)


def _moe_kernel(te_ref, tv_ref, h_ref, w1_ref, w3_ref, w2_ref, o_ref, acc_ref):
    r, j = pl.program_id(0), pl.program_id(1)
    last = pl.num_programs(1) - 1
    valid = tv_ref[r] > 0

    @pl.when(j == 0)
    def _():
        acc_ref[...] = jnp.zeros(acc_ref.shape, F32)

    @pl.when(valid)
    def _():
        h = h_ref[...]
        a = jnp.dot(h, w1_ref[...], preferred_element_type=F32)
        b = jnp.dot(h, w3_ref[...], preferred_element_type=F32)
        z = (a * jax.nn.sigmoid(a) * b).astype(BF16)
        acc_ref[...] += jnp.dot(z, w2_ref[...], preferred_element_type=F32)

    @pl.when(j == last)
    def _():
        o_ref[...] = acc_ref[...].astype(o_ref.dtype)


def _moe_ffn(h_sorted, tile_expert, tile_valid, w1, w3, w2, layer):
    s_pad, d = h_sorted.shape
    f = w1.shape[3]
    tm, tf = MOE_TILE, MOE_TILE_F
    n_tiles = s_pad // tm
    jf = lambda r, j, tv: jnp.where(tv[r] > 0, j, f // tf - 1)
    grid_spec = pltpu.PrefetchScalarGridSpec(
        num_scalar_prefetch=2,
        grid=(n_tiles, f // tf),
        in_specs=[
            pl.BlockSpec((tm, d), lambda r, j, te, tv: (r, 0)),
            pl.BlockSpec((None, None, d, tf), lambda r, j, te, tv: (layer, te[r], 0, jf(r, j, tv))),
            pl.BlockSpec((None, None, d, tf), lambda r, j, te, tv: (layer, te[r], 0, jf(r, j, tv))),
            pl.BlockSpec((None, None, tf, d), lambda r, j, te, tv: (layer, te[r], jf(r, j, tv), 0)),
        ],
        out_specs=pl.BlockSpec((tm, d), lambda r, j, te, tv: (r, 0)),
        scratch_shapes=[pltpu.VMEM((tm, d), F32)],
    )
    return pl.pallas_call(
        _moe_kernel,
        grid_spec=grid_spec,
        out_shape=jax.ShapeDtypeStruct((s_pad, d), BF16),
        compiler_params=_cparams("arbitrary", "arbitrary"),
        name="moe_experts",
    )(tile_expert, tile_valid, h_sorted, w1, w3, w2)


def _combine_kernel(x_ref, mod_ref, r_ref, y0_ref, y1_ref, o_ref):
    mix = r_ref[:, 2:3] * y0_ref[...].astype(F32) + r_ref[:, 3:4] * y1_ref[...].astype(F32)
    o_ref[...] = x_ref[...] + mod_ref[5:6, :] * mix


def _moe_combine(lay, xs, mod, route, y0, y1, first_tile):
    n, d = xs.shape
    row = pl.BlockSpec((ROW_TILE, d), lambda t: (t, 0))
    return pl.pallas_call(
        _combine_kernel,
        grid=(n // ROW_TILE,),
        in_specs=[row, pl.BlockSpec((None, N_MOD, d), lambda t: (lay.mod_row(t + first_tile), 0, 0)),
                  pl.BlockSpec((ROW_TILE, LANE), lambda t: (t, 0)), row, row],
        out_specs=row,
        out_shape=jax.ShapeDtypeStruct((n, d), F32),
        compiler_params=_cparams("parallel"),
        name="moe_combine",
    )(xs, mod, route, y0, y1)


def _moe(lay, h, route, xs, mod, w1, w3, w2, layer, first_tile):
    n, d = xs.shape
    tm = MOE_TILE
    n_assign = 2 * n
    n_tiles = n_assign // tm + N_EXPERTS
    s_pad = n_tiles * tm
    expert = route[:, 0:2].astype(jnp.int32).reshape(n_assign)
    onehot = (expert[:, None] == jnp.arange(N_EXPERTS)[None, :]).astype(jnp.int32)
    rank = jnp.take_along_axis(jnp.cumsum(onehot, axis=0) - onehot, expert[:, None], axis=1)[:, 0]
    count = jnp.sum(onehot, axis=0)
    padded = (count + tm - 1) // tm * tm
    end = jnp.cumsum(padded)
    start = end - padded
    slot = start[expert] + rank
    token = jnp.arange(n_assign, dtype=jnp.int32) // 2
    tok_of_slot = jnp.zeros((s_pad,), jnp.int32).at[slot].set(token)
    tile_start = jnp.arange(n_tiles, dtype=jnp.int32) * tm
    tile_expert = jnp.minimum(jnp.sum((tile_start[:, None] >= end[None, :]).astype(jnp.int32), axis=1),
                              N_EXPERTS - 1).astype(jnp.int32)
    tile_valid = (tile_start < end[-1]).astype(jnp.int32)
    h_sorted = h.at[tok_of_slot].get(mode="promise_in_bounds")
    y = _moe_ffn(h_sorted, tile_expert, tile_valid, w1, w3, w2, layer)
    slots = slot.reshape(n, 2)
    y0 = y.at[slots[:, 0]].get(mode="promise_in_bounds")
    y1 = y.at[slots[:, 1]].get(mode="promise_in_bounds")
    return _moe_combine(lay, xs, mod, route, y0, y1, first_tile)


def kernel(x, c, ctx, c_ctx, w_mod, b_mod, norm1_g, norm2_g, ev_w_in, ev_qn_g, ev_kn_g, s5_lam_re, s5_lam_im, s5_log_dt, s5_b_re, s5_b_im, s5_c_re, s5_c_im, s5_d, ev_w_glu, ev_w_out, ffn_w1, ffn_w3, ffn_w2, od_w_in, od_q_lora_g, od_kv_lora_g, od_w_uq, od_w_ukv, od_qn_g, od_kn_g, od_w_o, moe_router, moe_w1, moe_w3, moe_w2):
    bsz, seq, d = x.shape
    ctx_len = ctx.shape[1]
    depth = w_mod.shape[0]
    lay = _Layout(bsz, ctx_len, seq)

    cond = jnp.zeros((8, d), F32).at[:bsz].set(c).at[bsz].set(c_ctx)
    mods = _modulation(cond, w_mod, b_mod)

    cos_a, sin_a = _rope_tables(lay, A_HEAD_DIM)
    cos_c, sin_c = _rope_tables(lay, C_ROPE)
    cos_c = jnp.concatenate([cos_c, cos_c], -1)
    sin_c = jnp.concatenate([sin_c, sin_c], -1)

    nope_cols = (jnp.arange(C_HEADS)[:, None] * (C_NOPE + C_ROPE) + jnp.arange(C_NOPE)[None, :]).reshape(-1)
    rope_cols = (jnp.arange(C_HEADS)[:, None] * (C_NOPE + C_ROPE) + C_NOPE + jnp.arange(C_ROPE)[None, :]).reshape(-1)
    uq_cols = jnp.concatenate([nope_cols, rope_cols])

    moe_w1_b, moe_w3_b, moe_w2_b = moe_w1.astype(BF16), moe_w3.astype(BF16), moe_w2.astype(BF16)

    xs = jnp.concatenate([ctx.reshape(bsz * ctx_len, d), x.reshape(bsz * seq, d)], axis=0)
    for i in range(depth):
        j = i // 2
        mod = mods[i]
        if i % 2 == 0:
            q, k, vt, u, ub = _even_front(lay, xs, mod, norm1_g[i], ev_w_in[j].astype(BF16), ev_qn_g[j],
                                          ev_kn_g[j], cos_a, sin_a)
            attn = _attention(lay, q, k, vt, kv_heads=A_KV_HEADS, heads_per_step=1,
                              group=A_HEADS // A_KV_HEADS, dk=A_HEAD_DIM, dv=A_HEAD_DIM)
            yb = _s5_scan(lay, ub, s5_lam_re[j], s5_lam_im[j], s5_log_dt[j], s5_b_re[j], s5_b_im[j],
                          s5_c_re[j], s5_c_im[j])
            xs, h = _even_back(lay, attn, yb, u, xs, mod, s5_d[j], ev_w_glu[j].astype(BF16),
                               ev_w_out[j].astype(BF16), norm2_g[i])
            xs = _ffn(lay, h, xs, mod, ffn_w1[j].astype(BF16), ffn_w3[j].astype(BF16), ffn_w2[j].astype(BF16))
        else:
            w_in = jnp.pad(od_w_in[j], ((0, 0), (0, LANE - C_ROPE))).astype(BF16)
            q, k, vt = _odd_front(lay, xs, mod, norm1_g[i], w_in, od_q_lora_g[j], od_kv_lora_g[j],
                                  od_w_uq[j][:, uq_cols].astype(BF16), od_w_ukv[j].astype(BF16),
                                  od_qn_g[j], od_kn_g[j], cos_c, sin_c)
            attn = _attention(lay, q, k, vt, kv_heads=C_HEADS, heads_per_step=4, group=1, dk=2 * LANE, dv=C_V)
            first_tile = lay.ctx_tiles if i == depth - 1 else 0
            xs, h, route = _odd_back(lay, attn, xs, mod, od_w_o[j].astype(BF16), norm2_g[i], moe_router[j],
                                     first_tile)
            xs = _moe(lay, h, route, xs, mod, moe_w1_b, moe_w3_b, moe_w2_b, j, first_tile)
    if depth % 2 == 0:
        return xs.reshape(bsz, seq, d)
    return xs[bsz * ctx_len:].reshape(bsz, seq, d)
```
